```python
import math, functools
import jax, jax.numpy as jnp
from jax import lax
import numpy as np

D_MODEL = 1024
BATCH = 4
SEQ = 4096
DEPTH = 2
DEC_BATCH = 32
DEC_SEQ = 4
PAST_LEN = 16384
PAGE_SIZE = 128

HEAD_DIM = 64
MIX_WIDTH = D_MODEL
ATT_HEADS = 8
ATT_WIDTH = ATT_HEADS * HEAD_DIM
CONV_GROUPS = 8
CONV_WIDTH = MIX_WIDTH - ATT_WIDTH
CONV_K = 31
ROT_DIM = HEAD_DIM // 4
ROPE_THETA = 500000.0
MOBA_BLOCK = 256
MOBA_TOPK = 3
Q_CHUNK = 32
D_FF = 2816
FFN_CONV_K = 3
N_MOD = 6
EPS = 1e-6
IN_WIDTH = 3 * ATT_WIDTH + 2 * CONV_WIDTH

kernel_name = "moba_conformer_hymba_adaln_step"

F32 = jnp.float32


def rms_norm(x, g):
    xf = x.astype(F32)
    y = xf * lax.rsqrt(jnp.mean(xf * xf, axis=-1, keepdims=True) + EPS)
    return (y * g.astype(F32)).astype(x.dtype)


def layer_norm(x, g, b):
    xf = x.astype(F32)
    xc = xf - jnp.mean(xf, axis=-1, keepdims=True)
    y = xc * lax.rsqrt(jnp.mean(xc * xc, axis=-1, keepdims=True) + EPS)
    return (y * g.astype(F32) + b.astype(F32)).astype(x.dtype)


def rope_partial(x, pos):
    half = ROT_DIM // 2
    inv = jnp.exp(-math.log(ROPE_THETA) * jnp.arange(half, dtype=F32) * (2.0 / ROT_DIM))
    ang = pos.astype(F32)[:, None] * inv[None, :]
    cos = jnp.cos(ang)[:, None, :]
    sin = jnp.sin(ang)[:, None, :]
    xf = x.astype(F32)
    x1 = xf[..., :half]
    x2 = xf[..., half:ROT_DIM]
    out = jnp.concatenate([x1 * cos - x2 * sin, x2 * cos + x1 * sin, xf[..., ROT_DIM:]], axis=-1)
    return out.astype(x.dtype)


def causal_dwconv(x_ext, w, b):
    c = x_ext.shape[-1]
    y = lax.conv_general_dilated(x_ext, w[:, None, :].astype(x_ext.dtype), window_strides=(1,),
                                 padding='VALID', dimension_numbers=('NWC', 'WIO', 'NWC'),
                                 feature_group_count=c)
    return y + b


def modulation(c, w_ada, b_ada):
    m = jax.nn.silu(c) @ w_ada + b_ada
    return jnp.split(m[:, None, :], N_MOD, axis=-1)


def modulate(h, shift, scale):
    return h * (1.0 + scale) + shift


def block_attend(qh, ks, vs, sel_mask, ko, vo, own_mask):
    scale = HEAD_DIM ** -0.5
    s_own = jnp.einsum('bhqd,bhkd->bhqk', qh, ko, preferred_element_type=F32) * scale
    s_own = jnp.where(own_mask, s_own, -jnp.inf)
    if ks is None:
        p = jax.nn.softmax(s_own, axis=-1)
        return jnp.einsum('bhqk,bhkd->bhqd', p.astype(vo.dtype), vo)
    s_sel = jnp.einsum('bhqd,bhqkd->bhqk', qh, ks, preferred_element_type=F32) * scale
    if sel_mask is not None:
        s_sel = jnp.where(sel_mask, s_sel, -jnp.inf)
    p = jax.nn.softmax(jnp.concatenate([s_sel, s_own], axis=-1), axis=-1)
    ns = ks.shape[3]
    return (jnp.einsum('bhqk,bhqkd->bhqd', p[..., :ns].astype(vs.dtype), vs)
            + jnp.einsum('bhqk,bhkd->bhqd', p[..., ns:].astype(vo.dtype), vo))


def moba_prompt(q, k, v):
    b, s = q.shape[:2]
    nb = -(-s // MOBA_BLOCK)
    padw = ((0, 0), (0, nb * MOBA_BLOCK - s), (0, 0), (0, 0))
    kb = jnp.pad(k, padw).reshape(b, nb, MOBA_BLOCK, ATT_HEADS, HEAD_DIM).transpose(0, 3, 1, 2, 4)
    vb = jnp.pad(v, padw).reshape(b, nb, MOBA_BLOCK, ATT_HEADS, HEAD_DIM).transpose(0, 3, 1, 2, 4)
    kmean = jnp.mean(kb, axis=3, dtype=F32)
    qh = q.transpose(0, 2, 1, 3)
    n_sel = min(MOBA_TOPK, nb - 1)
    bi = jnp.arange(b)[:, None, None, None]
    hi = jnp.arange(ATT_HEADS)[None, :, None, None]
    blk_ids = jnp.arange(nb)

    def one_chunk(c):
        t0 = c * Q_CHUNK
        qc = lax.dynamic_slice_in_dim(qh, t0, Q_CHUNK, axis=2)
        qpos = t0 + jnp.arange(Q_CHUNK)
        blk = t0 // MOBA_BLOCK
        ko = lax.dynamic_index_in_dim(kb, blk, axis=2, keepdims=False)
        vo = lax.dynamic_index_in_dim(vb, blk, axis=2, keepdims=False)
        kpos = blk * MOBA_BLOCK + jnp.arange(MOBA_BLOCK)
        own_mask = kpos[None, :] <= qpos[:, None]
        if n_sel == 0:
            return block_attend(qc, None, None, None, ko, vo, own_mask)
        gs = jnp.einsum('bhqd,bhnd->bhqn', qc, kmean, preferred_element_type=F32)
        gs = jnp.where(blk_ids < blk, gs, -jnp.inf)
        _, idx = lax.top_k(gs, n_sel)
        sel_mask = jnp.repeat(idx < blk, MOBA_BLOCK, axis=-1)
        ks = kb[bi, hi, idx].reshape(b, ATT_HEADS, Q_CHUNK, n_sel * MOBA_BLOCK, HEAD_DIM)
        vs = vb[bi, hi, idx].reshape(b, ATT_HEADS, Q_CHUNK, n_sel * MOBA_BLOCK, HEAD_DIM)
        return block_attend(qc, ks, vs, sel_mask, ko, vo, own_mask)

    out = lax.map(one_chunk, jnp.arange(s // Q_CHUNK))
    return out.transpose(1, 0, 3, 2, 4).reshape(b, s, ATT_WIDTH)


def moba_sample(q, k, v, cache_k, cache_v, page_sums, page_table, layer):
    bd, sd = q.shape[:2]
    n_pages = page_table.shape[1]
    past = n_pages * PAGE_SIZE
    ppb = MOBA_BLOCK // PAGE_SIZE
    blk = past // MOBA_BLOCK
    r = past - blk * MOBA_BLOCK
    n_sel = min(MOBA_TOPK, blk)
    qh = q.transpose(0, 2, 1, 3)
    qpos = past + jnp.arange(sd)
    own_pt = page_table[:, blk * ppb:]
    ko = jnp.concatenate([cache_k[layer, own_pt].reshape(bd, r, ATT_HEADS, HEAD_DIM), k], axis=1).transpose(0, 2, 1, 3)
    vo = jnp.concatenate([cache_v[layer, own_pt].reshape(bd, r, ATT_HEADS, HEAD_DIM), v], axis=1).transpose(0, 2, 1, 3)
    kpos = blk * MOBA_BLOCK + jnp.arange(r + sd)
    own_mask = kpos[None, :] <= qpos[:, None]
    if n_sel == 0:
        o = block_attend(qh, None, None, None, ko, vo, own_mask)
    else:
        bsum = page_sums[layer][page_table[:, :blk * ppb]]
        kmean = (bsum.reshape(bd, blk, ppb, ATT_HEADS, HEAD_DIM).sum(axis=2) / MOBA_BLOCK).transpose(0, 2, 1, 3)
        gs = jnp.einsum('bhqd,bhnd->bhqn', qh, kmean, preferred_element_type=F32)
        _, idx = lax.top_k(gs, n_sel)
        logical = idx[..., None] * ppb + jnp.arange(ppb)
        phys = page_table[jnp.arange(bd)[:, None, None, None, None], logical]
        hi = jnp.arange(ATT_HEADS)[None, :, None, None, None, None]
        tok = jnp.arange(PAGE_SIZE)
        ks = cache_k[layer, phys[..., None], tok, hi].reshape(bd, ATT_HEADS, sd, n_sel * MOBA_BLOCK, HEAD_DIM)
        vs = cache_v[layer, phys[..., None], tok, hi].reshape(bd, ATT_HEADS, sd, n_sel * MOBA_BLOCK, HEAD_DIM)
        o = block_attend(qh, ks, vs, None, ko, vo, own_mask)
    return o.transpose(0, 2, 1, 3).reshape(bd, sd, ATT_WIDTH)


def mixer_block(x, shift, scale, gate, norm_g, w_in, q_g, k_g, conv_w, conv_b, ln_g, ln_b,
                att_g, conv_g, w_out, pos, conv_prefix, attend):
    bsz, L = x.shape[:2]
    h = modulate(rms_norm(x, norm_g), shift, scale)
    z = h @ w_in
    q, k, v, a, g = jnp.split(z, [ATT_WIDTH, 2 * ATT_WIDTH, 3 * ATT_WIDTH, 3 * ATT_WIDTH + CONV_WIDTH], axis=-1)
    q = rope_partial(rms_norm(q.reshape(bsz, L, ATT_HEADS, HEAD_DIM), q_g), pos)
    k = rope_partial(rms_norm(k.reshape(bsz, L, ATT_HEADS, HEAD_DIM), k_g), pos)
    v = v.reshape(bsz, L, ATT_HEADS, HEAD_DIM)
    att = attend(q, k, v)
    u_ext = jnp.concatenate([conv_prefix.astype(a.dtype), a * jax.nn.sigmoid(g)], axis=1)
    cv = jax.nn.silu(layer_norm(causal_dwconv(u_ext, conv_w, conv_b), ln_g, ln_b))
    merged = jnp.concatenate([rms_norm(att, att_g), rms_norm(cv, conv_g)], axis=-1) @ w_out
    return x + gate * merged, k, v, u_ext[:, -(CONV_K - 1):]


def ffn_block(x, shift, scale, gate, norm_g, w_fc1, cw, cb, w_fc2, prefix):
    h = modulate(rms_norm(x, norm_g), shift, scale)
    g, up = jnp.split(h @ w_fc1, 2, axis=-1)
    g_ext = jnp.concatenate([prefix.astype(g.dtype), g], axis=1)
    act = jax.nn.silu(causal_dwconv(g_ext, cw, cb))
    return x + gate * ((act * up) @ w_fc2), g_ext[:, -(FFN_CONV_K - 1):]


def setup_inputs(seed: int = 0) -> dict:
    key = jax.random.key(seed)
    ks = jax.random.split(key, 32)
    n_pages = PAST_LEN // PAGE_SIZE
    n_used = DEC_BATCH * n_pages
    n_pool = n_used + max(1, n_used // 4)

    def nrm(k, shape, s):
        return jax.random.normal(k, shape, F32) * s

    page_table = jax.random.permutation(ks[6], n_pool)[:n_used].reshape(DEC_BATCH, n_pages).astype(jnp.int32)
    return {
        "x_prompt": nrm(ks[0], (BATCH, SEQ, D_MODEL), 1.0),
        "x_sample": nrm(ks[1], (DEC_BATCH, DEC_SEQ, D_MODEL), 1.0),
        "cache_k": nrm(ks[2], (DEPTH, n_pool, PAGE_SIZE, ATT_HEADS, HEAD_DIM), 1.0),
        "cache_v": nrm(ks[3], (DEPTH, n_pool, PAGE_SIZE, ATT_HEADS, HEAD_DIM), 1.0),
        "state_conv": nrm(ks[4], (DEPTH, DEC_BATCH, CONV_K - 1, CONV_WIDTH), 0.5),
        "state_ffn": nrm(ks[5], (DEPTH, DEC_BATCH, FFN_CONV_K - 1, D_FF), 1.0),
        "page_table": page_table,
        "c_prompt": nrm(ks[7], (BATCH, D_MODEL), 1.0),
        "c_sample": nrm(ks[8], (DEC_BATCH, D_MODEL), 1.0),
        "norm1_g": 1.0 + nrm(ks[9], (DEPTH, D_MODEL), 0.05),
        "norm2_g": 1.0 + nrm(ks[10], (DEPTH, D_MODEL), 0.05),
        "w_ada": nrm(ks[11], (DEPTH, D_MODEL, N_MOD * D_MODEL), 0.5 * D_MODEL ** -0.5),
        "b_ada": nrm(ks[12], (DEPTH, N_MOD * D_MODEL), 0.02),
        "w_in": nrm(ks[13], (DEPTH, D_MODEL, IN_WIDTH), D_MODEL ** -0.5),
        "q_norm_g": 1.0 + nrm(ks[14], (DEPTH, HEAD_DIM), 0.05),
        "k_norm_g": 1.0 + nrm(ks[15], (DEPTH, HEAD_DIM), 0.05),
        "conv_w": nrm(ks[16], (DEPTH, CONV_K, CONV_WIDTH), CONV_K ** -0.5),
        "conv_b": nrm(ks[17], (DEPTH, CONV_WIDTH), 0.02),
        "conv_ln_g": 1.0 + nrm(ks[18], (DEPTH, CONV_WIDTH), 0.05),
        "conv_ln_b": nrm(ks[19], (DEPTH, CONV_WIDTH), 0.02),
        "attn_out_g": 1.0 + nrm(ks[20], (DEPTH, ATT_WIDTH), 0.05),
        "conv_out_g": 1.0 + nrm(ks[21], (DEPTH, CONV_WIDTH), 0.05),
        "w_out": nrm(ks[22], (DEPTH, MIX_WIDTH, D_MODEL), MIX_WIDTH ** -0.5),
        "w_fc1": nrm(ks[23], (DEPTH, D_MODEL, 2 * D_FF), D_MODEL ** -0.5),
        "ffn_conv_w": nrm(ks[24], (DEPTH, FFN_CONV_K, D_FF), FFN_CONV_K ** -0.5),
        "ffn_conv_b": nrm(ks[25], (DEPTH, D_FF), 0.02),
        "w_fc2": nrm(ks[26], (DEPTH, D_FF, D_MODEL), D_FF ** -0.5),
    }


def reference(x_prompt, x_sample, cache_k, cache_v, state_conv, state_ffn, page_table, c_prompt, c_sample,
              norm1_g, norm2_g, w_ada, b_ada, w_in, q_norm_g, k_norm_g, conv_w, conv_b, conv_ln_g, conv_ln_b,
              attn_out_g, conv_out_g, w_out, w_fc1, ffn_conv_w, ffn_conv_b, w_fc2):
    b, s = x_prompt.shape[:2]
    sd = x_sample.shape[1]
    past = page_table.shape[1] * PAGE_SIZE
    pos_p = jnp.arange(s)
    pos_s = past + jnp.arange(sd)
    page_sums = jnp.sum(cache_k, axis=2, dtype=F32)
    conv_zero = jnp.zeros((b, CONV_K - 1, CONV_WIDTH), x_prompt.dtype)
    ffn_zero = jnp.zeros((b, FFN_CONV_K - 1, D_FF), x_prompt.dtype)
    xp, xs = x_prompt, x_sample
    kp, vp, cp, fp, ksl, vsl, csl, fsl = [], [], [], [], [], [], [], []
    for l in range(DEPTH):
        mix_w = (norm1_g[l], w_in[l], q_norm_g[l], k_norm_g[l], conv_w[l], conv_b[l], conv_ln_g[l],
                 conv_ln_b[l], attn_out_g[l], conv_out_g[l], w_out[l])
        ffn_w = (norm2_g[l], w_fc1[l], ffn_conv_w[l], ffn_conv_b[l], w_fc2[l])
        mp = modulation(c_prompt, w_ada[l], b_ada[l])
        ms = modulation(c_sample, w_ada[l], b_ada[l])
        attend_s = functools.partial(moba_sample, cache_k=cache_k, cache_v=cache_v, page_sums=page_sums,
                                     page_table=page_table, layer=l)
        xp, k_, v_, st = mixer_block(xp, mp[0], mp[1], mp[2], *mix_w, pos_p, conv_zero, moba_prompt)
        kp.append(k_); vp.append(v_); cp.append(st)
        xs, k_, v_, st = mixer_block(xs, ms[0], ms[1], ms[2], *mix_w, pos_s, state_conv[l], attend_s)
        ksl.append(k_); vsl.append(v_); csl.append(st)
        xp, st = ffn_block(xp, mp[3], mp[4], mp[5], *ffn_w, ffn_zero)
        fp.append(st)
        xs, st = ffn_block(xs, ms[3], ms[4], ms[5], *ffn_w, state_ffn[l])
        fsl.append(st)
    return (xp, xs, jnp.stack(kp), jnp.stack(vp), jnp.stack(cp), jnp.stack(fp),
            jnp.stack(ksl), jnp.stack(vsl), jnp.stack(csl), jnp.stack(fsl))
```

```python
import functools
import math

import jax
import jax.numpy as jnp
from jax import lax
from jax.experimental import pallas as pl
from jax.experimental.pallas import tpu as pltpu

F32 = jnp.float32
BF16 = jnp.bfloat16

D_MODEL = 1024
HEAD_DIM = 64
ATT_HEADS = 8
ATT_WIDTH = 512
CONV_WIDTH = 512
CONV_K = 31
ROT_DIM = 16
ROPE_THETA = 500000.0
MOBA_BLOCK = 256
MOBA_TOPK = 3
PAGE_SIZE = 128
D_FF = 2816
FFN_CONV_K = 3
N_MOD = 6
EPS = 1e-6
IN_WIDTH = 3 * ATT_WIDTH + 2 * CONV_WIDTH

LANES = 128
NEG_BIG = -1e30
FF_CHUNK = 256
N_FF_CHUNKS = D_FF // FF_CHUNK
PAGES_PER_STEP = 32

TM_IN = 512
TM_FFN = 512
TC_CONV = 256
CONV_HALO = 32
CONV_ROWS = 64


def _dot(a, b):
    return jnp.dot(a, b, preferred_element_type=F32)


def _dot_nt(a, b):
    return lax.dot_general(a, b, (((1,), (1,)), ((), ())), preferred_element_type=F32)


def _split(a):
    hi = a.astype(BF16)
    lo = (a - hi.astype(F32)).astype(BF16)
    return hi, lo


def _dot3(a, b):
    ah, al = _split(a)
    bh, bl = _split(b)
    return _dot(ah, bh) + _dot(al, bh) + _dot(ah, bl)


def _dot3_nt(a, b):
    ah, al = _split(a)
    bh, bl = _split(b)
    return _dot_nt(ah, bh) + _dot_nt(al, bh) + _dot_nt(ah, bl)


def _sigmoid(x):
    return 1.0 / (1.0 + jnp.exp(-x))


def _silu(x):
    return x * _sigmoid(x)


def _rms(x, g):
    ms = jnp.mean(x * x, axis=-1, keepdims=True)
    return x * lax.rsqrt(ms + EPS) * g


def _mod_kernel(c_ref, w_ref, b_ref, o_ref):
    a = _silu(c_ref[...])
    o_ref[0] = _dot3(a, w_ref[0]) + b_ref[0]


def _modulation(c_all, w_ada, b_ada):
    depth = w_ada.shape[0]
    rows = c_all.shape[0]
    nt = (N_MOD * D_MODEL) // D_MODEL
    return pl.pallas_call(
        _mod_kernel,
        grid=(depth, nt),
        in_specs=[
            pl.BlockSpec((rows, D_MODEL), lambda l, n: (0, 0)),
            pl.BlockSpec((1, D_MODEL, D_MODEL), lambda l, n: (l, 0, n)),
            pl.BlockSpec((1, 1, D_MODEL), lambda l, n: (l, 0, n)),
        ],
        out_specs=pl.BlockSpec((1, rows, D_MODEL), lambda l, n: (l, 0, n)),
        out_shape=jax.ShapeDtypeStruct((depth, rows, N_MOD * D_MODEL), F32),
        name="modulation",
    )(c_all, w_ada, b_ada.reshape(depth, 1, N_MOD * D_MODEL))


def _tile_lanes(t, n):
    return jnp.concatenate([t] * n, axis=1)


def _mixer_in_kernel(x_ref, sh_ref, sc_ref, ng_ref, w_ref, bd_ref, qg_ref, kg_ref,
                     cos_ref, s1_ref, s2_ref, q_ref, k_ref, v_ref, u_ref, *extra,
                     prompt, tm, tiles_per_seq):
    x = x_ref[...]
    h = _rms(x, ng_ref[...])
    h = h * (1.0 + sc_ref[0]) + sh_ref[0]
    z = _dot(h.astype(BF16), w_ref[...])

    reps = ATT_WIDTH // LANES
    cos = _tile_lanes(cos_ref[...], reps)
    s1 = _tile_lanes(s1_ref[...], reps)
    s2 = _tile_lanes(s2_ref[...], reps)
    bd = bd_ref[...]

    def head_norm_rope(t, g):
        hi, lo = _split(t * t)
        ms = _dot(hi, bd) + _dot(lo, bd)
        tn = t * lax.rsqrt(ms + EPS) * g
        return (tn * cos + pltpu.roll(tn, ATT_WIDTH - ROT_DIM // 2, 1) * s1
                + pltpu.roll(tn, ROT_DIM // 2, 1) * s2)

    q = head_norm_rope(z[:, 0:ATT_WIDTH], qg_ref[...])
    k = head_norm_rope(z[:, ATT_WIDTH:2 * ATT_WIDTH], kg_ref[...])
    v = z[:, 2 * ATT_WIDTH:3 * ATT_WIDTH]
    a = z[:, 3 * ATT_WIDTH:3 * ATT_WIDTH + CONV_WIDTH]
    g = z[:, 3 * ATT_WIDTH + CONV_WIDTH:]
    q_ref[...] = q
    k_ref[...] = k
    v_ref[...] = v
    u_ref[...] = a * _sigmoid(g)

    if prompt:
        ka_ref, va_ref, km_ref = extra
        t_in_seq = pl.program_id(0) % tiles_per_seq
        lane = lax.broadcasted_iota(jnp.int32, (tm, LANES), 1)
        row = lax.broadcasted_iota(jnp.int32, (tm, LANES), 0)
        blk = (t_in_seq * tm + row) // MOBA_BLOCK
        onehot = jnp.where(lane - HEAD_DIM == blk, 1.0, 0.0)
        for hd in range(ATT_HEADS):
            lo_l = LANES * (hd // 2)
            kt = k[:, lo_l:lo_l + LANES]
            vt = v[:, lo_l:lo_l + LANES]
            if hd % 2 == 1:
                kt = pltpu.roll(kt, HEAD_DIM, 1)
                vt = pltpu.roll(vt, HEAD_DIM, 1)
            ka_ref[0, hd] = jnp.where(lane < HEAD_DIM, kt, onehot).astype(BF16)
            va_ref[0, hd] = jnp.where(lane < HEAD_DIM, vt, 1.0).astype(BF16)
        for c in range(tm // MOBA_BLOCK):
            km_ref[c] = jnp.sum(k[c * MOBA_BLOCK:(c + 1) * MOBA_BLOCK], axis=0, keepdims=True) * (1.0 / MOBA_BLOCK)


def _mixer_in(x, shift, scale, norm_g, w_in_bf, bd, qg, kg, cos_t, s1_t, s2_t, *, prompt, tm, seq):
    n = x.shape[0]
    nt = n // tm
    tiles_per_seq = seq // tm
    mod_rows = shift.shape[1]
    if prompt:
        mod_map = lambda t: (t // tiles_per_seq, 0, 0)
        tab_map = lambda t: (t % tiles_per_seq, 0)
    else:
        mod_map = lambda t: (t, 0, 0)
        tab_map = lambda t: (t, 0)
    const = lambda t: (0, 0)
    row_map = lambda t: (t, 0)
    in_specs = [
        pl.BlockSpec((tm, D_MODEL), row_map),
        pl.BlockSpec((1, mod_rows, D_MODEL), mod_map),
        pl.BlockSpec((1, mod_rows, D_MODEL), mod_map),
        pl.BlockSpec((1, D_MODEL), const),
        pl.BlockSpec((D_MODEL, IN_WIDTH), const),
        pl.BlockSpec((ATT_WIDTH, ATT_WIDTH), const),
        pl.BlockSpec((1, ATT_WIDTH), const),
        pl.BlockSpec((1, ATT_WIDTH), const),
        pl.BlockSpec((tm, LANES), tab_map),
        pl.BlockSpec((tm, LANES), tab_map),
        pl.BlockSpec((tm, LANES), tab_map),
    ]
    out_specs = [pl.BlockSpec((tm, ATT_WIDTH), row_map)] * 4
    out_shape = [jax.ShapeDtypeStruct((n, ATT_WIDTH), F32)] * 4
    if prompt:
        nb = n // seq
        aug_map = lambda t: (t // tiles_per_seq, 0, t % tiles_per_seq, 0)
        out_specs += [pl.BlockSpec((1, ATT_HEADS, tm, LANES), aug_map)] * 2
        out_shape += [jax.ShapeDtypeStruct((nb, ATT_HEADS, seq, LANES), BF16)] * 2
        bpt = tm // MOBA_BLOCK
        out_specs += [pl.BlockSpec((bpt, 1, ATT_WIDTH), lambda t: (t, 0, 0))]
        out_shape += [jax.ShapeDtypeStruct((n // MOBA_BLOCK, 1, ATT_WIDTH), F32)]
    return pl.pallas_call(
        functools.partial(_mixer_in_kernel, prompt=prompt, tm=tm, tiles_per_seq=tiles_per_seq),
        grid=(nt,),
        in_specs=in_specs,
        out_specs=out_specs,
        out_shape=out_shape,
        compiler_params=pltpu.CompilerParams(dimension_semantics=("arbitrary",)),
        name="mixer_in_prompt" if prompt else "mixer_in_sample",
    )(x, shift, scale, norm_g, w_in_bf, bd, qg, kg, cos_t, s1_t, s2_t)


def _top3_select(cur, lanef):
    selm = jnp.zeros(cur.shape, F32)
    firsts = []
    for _ in range(MOBA_TOPK):
        m = jnp.max(cur, axis=1, keepdims=True)
        first = jnp.min(jnp.where(cur == m, lanef, 1e9), axis=1, keepdims=True)
        pick = (lanef == first) & (m > -jnp.inf)
        selm = jnp.where(pick, 1.0, selm)
        cur = jnp.where(pick, -jnp.inf, cur)
        firsts.append(first)
    return selm, firsts


def _attn_kernel(q_ref, ka_ref, va_ref, kmw_ref, o_ref):
    i = pl.program_id(2)
    tq = MOBA_BLOCK
    q2 = q_ref[0]
    lane = lax.broadcasted_iota(jnp.int32, (tq, LANES), 1)
    lanef = lane.astype(F32)
    row = lax.broadcasted_iota(jnp.int32, (tq, tq), 0)
    col = lax.broadcasted_iota(jnp.int32, (tq, tq), 1)
    causal = col <= row
    valid = (lane >= HEAD_DIM) & (lane < HEAD_DIM + i)

    qas = []
    for hh in range(2):
        qh = q2 if hh == 0 else pltpu.roll(q2, HEAD_DIM, 1)
        qz = jnp.where(lane < HEAD_DIM, qh, 0.0)
        gs = _dot3_nt(qz, kmw_ref[0, hh])
        selm, _ = _top3_select(jnp.where(valid, gs, -jnp.inf), lanef)
        bias = jnp.where(valid & (selm == 0.0), NEG_BIG, 0.0)
        qas.append(jnp.where(lane < HEAD_DIM, qh * (HEAD_DIM ** -0.5), bias).astype(BF16))

    def block_rows(j):
        return pl.ds(pl.multiple_of(j * tq, tq), tq)

    init = []
    for hh in range(2):
        s = _dot_nt(qas[hh], ka_ref[0, hh, block_rows(i), :])
        s = jnp.where(causal, s, -jnp.inf)
        m = jnp.max(s, axis=1, keepdims=True)
        p = jnp.exp(s - m)
        acc = _dot(p.astype(BF16), va_ref[0, hh, block_rows(i), :])
        init.append((m, acc))

    def body(j, carry):
        new = []
        for hh in range(2):
            m, acc = carry[hh]
            s = _dot_nt(qas[hh], ka_ref[0, hh, block_rows(j), :])
            mn = jnp.maximum(m, jnp.max(s, axis=1, keepdims=True))
            p = jnp.exp(s - mn)
            acc = acc * jnp.exp(m - mn) + _dot(p.astype(BF16), va_ref[0, hh, block_rows(j), :])
            new.append((mn, acc))
        return tuple(new)

    (_, acc0), (_, acc1) = lax.fori_loop(0, i, body, tuple(init))
    o0 = acc0 / pltpu.roll(acc0, HEAD_DIM, 1)
    o1 = pltpu.roll(acc1, HEAD_DIM, 1) / acc1
    o_ref[0] = jnp.where(lane < HEAD_DIM, o0, o1)


def _attention_prompt(q, kaug, vaug, kmw, *, nb, seq):
    nq = seq // MOBA_BLOCK
    hp = ATT_HEADS // 2
    return pl.pallas_call(
        _attn_kernel,
        grid=(nb, hp, nq),
        in_specs=[
            pl.BlockSpec((1, MOBA_BLOCK, LANES), lambda b, h, i: (b, i, h)),
            pl.BlockSpec((1, 2, seq, LANES), lambda b, h, i: (b, h, 0, 0)),
            pl.BlockSpec((1, 2, seq, LANES), lambda b, h, i: (b, h, 0, 0)),
            pl.BlockSpec((1, 2, LANES, LANES), lambda b, h, i: (b, h, 0, 0)),
        ],
        out_specs=pl.BlockSpec((1, MOBA_BLOCK, LANES), lambda b, h, i: (b, i, h)),
        out_shape=jax.ShapeDtypeStruct((nb, seq, ATT_WIDTH), F32),
        compiler_params=pltpu.CompilerParams(dimension_semantics=("arbitrary", "arbitrary", "arbitrary")),
        name="attention_prompt",
    )(q.reshape(nb, seq, ATT_WIDTH), kaug, vaug, kmw)


def _conv_post(y, lg, lb, cg):
    mu = jnp.mean(y, axis=-1, keepdims=True)
    yc = y - mu
    var = jnp.mean(yc * yc, axis=-1, keepdims=True)
    cv = _silu(yc * lax.rsqrt(var + EPS) * lg + lb)
    return _rms(cv, cg)


def _conv_kernel(prev_ref, cur_ref, w_ref, b_ref, lg_ref, lb_ref, cg_ref, o_ref, ext_ref, *, tc):
    t = pl.program_id(1)
    ext_ref[0:CONV_HALO, :] = jnp.where(t > 0, prev_ref[0], 0.0)
    ext_ref[CONV_HALO:CONV_HALO + tc, :] = cur_ref[0]
    off = CONV_HALO - (CONV_K - 1)
    for c in range(tc // CONV_ROWS):
        acc = jnp.zeros((CONV_ROWS, CONV_WIDTH), F32)
        for j in range(CONV_K):
            acc = acc + w_ref[j:j + 1, :] * ext_ref[pl.ds(c * CONV_ROWS + off + j, CONV_ROWS), :]
        y = acc + b_ref[...]
        o_ref[0, c * CONV_ROWS:(c + 1) * CONV_ROWS, :] = _conv_post(
            y, lg_ref[...], lb_ref[...], cg_ref[...]).astype(o_ref.dtype)


def _conv_prompt(u, conv_w, conv_b, ln_g, ln_b, cg, *, nb, seq):
    tc = TC_CONV
    nt = seq // tc
    hpt = tc // CONV_HALO
    vec = pl.BlockSpec((1, CONV_WIDTH), lambda b, t: (0, 0))
    u3 = u.reshape(nb, seq, CONV_WIDTH)
    return pl.pallas_call(
        functools.partial(_conv_kernel, tc=tc),
        grid=(nb, nt),
        in_specs=[
            pl.BlockSpec((1, CONV_HALO, CONV_WIDTH), lambda b, t: (b, jnp.maximum(t * hpt - 1, 0), 0)),
            pl.BlockSpec((1, tc, CONV_WIDTH), lambda b, t: (b, t, 0)),
            pl.BlockSpec((CONV_K, CONV_WIDTH), lambda b, t: (0, 0)),
            vec, vec, vec, vec,
        ],
        out_specs=pl.BlockSpec((1, tc, CONV_WIDTH), lambda b, t: (b, t, 0)),
        out_shape=jax.ShapeDtypeStruct((nb, seq, CONV_WIDTH), BF16),
        scratch_shapes=[pltpu.VMEM((CONV_HALO + tc, CONV_WIDTH), F32)],
        compiler_params=pltpu.CompilerParams(dimension_semantics=("arbitrary", "arbitrary")),
        name="conv_prompt",
    )(u3, u3, conv_w, conv_b, ln_g, ln_b, cg)


def _conv_sample_kernel(ue_ref, w_ref, b_ref, lg_ref, lb_ref, cg_ref, o_ref, *, rows):
    ue = ue_ref[0]
    w = w_ref[...]
    ys = [jnp.sum(ue[t:t + CONV_K, :] * w, axis=0, keepdims=True) for t in range(rows)]
    y = jnp.concatenate(ys, axis=0) + b_ref[...]
    o_ref[0] = _conv_post(y, lg_ref[...], lb_ref[...], cg_ref[...]).astype(o_ref.dtype)


def _conv_sample(u_ext, conv_w, conv_b, ln_g, ln_b, cg):
    nb, ext_rows, _ = u_ext.shape
    rows = ext_rows - (CONV_K - 1)
    vec = pl.BlockSpec((1, CONV_WIDTH), lambda b: (0, 0))
    return pl.pallas_call(
        functools.partial(_conv_sample_kernel, rows=rows),
        grid=(nb,),
        in_specs=[
            pl.BlockSpec((1, ext_rows, CONV_WIDTH), lambda b: (b, 0, 0)),
            pl.BlockSpec((CONV_K, CONV_WIDTH), lambda b: (0, 0)),
            vec, vec, vec, vec,
        ],
        out_specs=pl.BlockSpec((1, rows, CONV_WIDTH), lambda b: (b, 0, 0)),
        out_shape=jax.ShapeDtypeStruct((nb, rows, CONV_WIDTH), F32),
        compiler_params=pltpu.CompilerParams(dimension_semantics=("arbitrary",)),
        name="conv_sample",
    )(u_ext, conv_w, conv_b, ln_g, ln_b, cg)


def _merge(x, att, cvn, g1, ag, wo_ref):
    attn = _rms(att, ag).astype(BF16)
    merged = _dot(attn, wo_ref[0:ATT_WIDTH, :]) + _dot(cvn.astype(BF16), wo_ref[ATT_WIDTH:, :])
    return x + g1 * merged


def _ffn_prompt_kernel(x_ref, att_ref, cvn_ref, g1_ref, sh_ref, sc_ref, g2_ref, ag_ref, ng_ref,
                       wo_ref, w1_ref, cw_ref, cb_ref, w2_ref, o_ref, st_ref, carry_ref, act_ref,
                       *, tm, tiles_per_seq):
    t_in_seq = pl.program_id(0) % tiles_per_seq

    @pl.when(t_in_seq == 0)
    def _():
        carry_ref[...] = jnp.zeros(carry_ref.shape, F32)

    x1 = _merge(x_ref[...], att_ref[...], cvn_ref[...], g1_ref[0], ag_ref[...], wo_ref)
    h2 = (_rms(x1, ng_ref[...]) * (1.0 + sc_ref[0]) + sh_ref[0]).astype(BF16)
    row = lax.broadcasted_iota(jnp.int32, (tm, FF_CHUNK), 0)
    for c in range(N_FF_CHUNKS):
        cols = slice(c * FF_CHUNK, (c + 1) * FF_CHUNK)
        g = _dot(h2, w1_ref[:, cols])
        up = _dot(h2, w1_ref[:, D_FF + c * FF_CHUNK:D_FF + (c + 1) * FF_CHUNK])
        p0 = carry_ref[0:1, cols]
        p1 = carry_ref[1:2, cols]
        gm1 = jnp.where(row == 0, p1, pltpu.roll(g, 1, 0))
        gm2 = jnp.where(row == 0, p0, jnp.where(row == 1, p1, pltpu.roll(g, 2, 0)))
        y = cw_ref[0:1, cols] * gm2 + cw_ref[1:2, cols] * gm1 + cw_ref[2:3, cols] * g + cb_ref[:, cols]
        act_ref[:, cols] = (_silu(y) * up).astype(BF16)
        tail = g[tm - (FFN_CONV_K - 1):tm, :]
        carry_ref[0:FFN_CONV_K - 1, cols] = tail
        st_ref[0, :, cols] = tail
    o_ref[...] = x1 + g2_ref[0] * _dot(act_ref[...], w2_ref[...])


def _ffn_prompt(x, att, cvn, gate1, shift2, scale2, gate2, att_g, norm2_g, wo_bf, w1_bf, cw, cb, w2_bf,
                *, nb, seq):
    tm = TM_FFN
    n = x.shape[0]
    nt = n // tm
    tiles_per_seq = seq // tm
    row_map = lambda t: (t, 0)
    mod_map = lambda t: (t // tiles_per_seq, 0, 0)
    const = lambda t: (0, 0)
    mod = pl.BlockSpec((1, 1, D_MODEL), mod_map)

    def resident(shape):
        return pl.BlockSpec(shape, const, pipeline_mode=pl.Buffered(1))

    return pl.pallas_call(
        functools.partial(_ffn_prompt_kernel, tm=tm, tiles_per_seq=tiles_per_seq),
        grid=(nt,),
        in_specs=[
            pl.BlockSpec((tm, D_MODEL), row_map),
            pl.BlockSpec((tm, ATT_WIDTH), row_map),
            pl.BlockSpec((tm, CONV_WIDTH), row_map),
            mod, mod, mod, mod,
            pl.BlockSpec((1, ATT_WIDTH), const),
            pl.BlockSpec((1, D_MODEL), const),
            resident((D_MODEL, D_MODEL)),
            resident((D_MODEL, 2 * D_FF)),
            pl.BlockSpec((FFN_CONV_K, D_FF), const),
            pl.BlockSpec((1, D_FF), const),
            resident((D_FF, D_MODEL)),
        ],
        out_specs=[
            pl.BlockSpec((tm, D_MODEL), row_map),
            pl.BlockSpec((1, FFN_CONV_K - 1, D_FF), mod_map),
        ],
        out_shape=[
            jax.ShapeDtypeStruct((n, D_MODEL), F32),
            jax.ShapeDtypeStruct((nb, FFN_CONV_K - 1, D_FF), F32),
        ],
        scratch_shapes=[
            pltpu.VMEM((8, D_FF), F32),
            pltpu.VMEM((tm, D_FF), BF16),
        ],
        compiler_params=pltpu.CompilerParams(dimension_semantics=("arbitrary",)),
        name="ffn_prompt",
    )(x, att, cvn, gate1, shift2, scale2, gate2, att_g, norm2_g, wo_bf, w1_bf, cw, cb, w2_bf)


def _ffn_sample_kernel(x_ref, att_ref, cvn_ref, g1_ref, sh_ref, sc_ref, g2_ref, ag_ref, ng_ref,
                       wo_ref, w1g_ref, w1u_ref, cw_ref, cb_ref, w2_ref, pm1_ref, pm2_ref,
                       o_ref, gout_ref, x1_ref, h2_ref, acc_ref, *, rows, seq):
    c = pl.program_id(0)

    @pl.when(c == 0)
    def _():
        x1 = _merge(x_ref[...], att_ref[...], cvn_ref[...], g1_ref[...], ag_ref[...], wo_ref)
        x1_ref[...] = x1
        h2_ref[...] = (_rms(x1, ng_ref[...]) * (1.0 + sc_ref[...]) + sh_ref[...]).astype(BF16)
        acc_ref[...] = jnp.zeros(acc_ref.shape, F32)

    h2 = h2_ref[...]
    g = _dot(h2, w1g_ref[...])
    up = _dot(h2, w1u_ref[...])
    t = lax.broadcasted_iota(jnp.int32, (rows, FF_CHUNK), 0) % seq
    gm1 = jnp.where(t == 0, pm1_ref[...], pltpu.roll(g, 1, 0))
    gm2 = jnp.where(t < 2, pm2_ref[...], pltpu.roll(g, 2, 0))
    y = cw_ref[0:1, :] * gm2 + cw_ref[1:2, :] * gm1 + cw_ref[2:3, :] * g + cb_ref[...]
    act = (_silu(y) * up).astype(BF16)
    acc_ref[...] += _dot(act, w2_ref[...])
    gout_ref[...] = g

    @pl.when(c == pl.num_programs(0) - 1)
    def _():
        o_ref[...] = x1_ref[...] + g2_ref[...] * acc_ref[...]


def _ffn_sample(x, att, cvn, gate1, shift2, scale2, gate2, att_g, norm2_g, wo_bf, w1_bf, cw, cb, w2_bf,
                pm1, pm2, *, seq):
    rows = x.shape[0]
    const = lambda c: (0, 0)
    full = lambda w: pl.BlockSpec((rows, w), const)
    chunk = lambda c: (0, c)
    return pl.pallas_call(
        functools.partial(_ffn_sample_kernel, rows=rows, seq=seq),
        grid=(N_FF_CHUNKS,),
        in_specs=[
            full(D_MODEL), full(ATT_WIDTH), full(CONV_WIDTH),
            full(D_MODEL), full(D_MODEL), full(D_MODEL), full(D_MODEL),
            pl.BlockSpec((1, ATT_WIDTH), const),
            pl.BlockSpec((1, D_MODEL), const),
            pl.BlockSpec((D_MODEL, D_MODEL), const),
            pl.BlockSpec((D_MODEL, FF_CHUNK), chunk),
            pl.BlockSpec((D_MODEL, FF_CHUNK), lambda c: (0, N_FF_CHUNKS + c)),
            pl.BlockSpec((FFN_CONV_K, FF_CHUNK), chunk),
            pl.BlockSpec((1, FF_CHUNK), chunk),
            pl.BlockSpec((FF_CHUNK, D_MODEL), lambda c: (c, 0)),
            pl.BlockSpec((rows, FF_CHUNK), chunk),
            pl.BlockSpec((rows, FF_CHUNK), chunk),
        ],
        out_specs=[
            pl.BlockSpec((rows, D_MODEL), const),
            pl.BlockSpec((rows, FF_CHUNK), chunk),
        ],
        out_shape=[
            jax.ShapeDtypeStruct((rows, D_MODEL), F32),
            jax.ShapeDtypeStruct((rows, D_FF), F32),
        ],
        scratch_shapes=[
            pltpu.VMEM((rows, D_MODEL), F32),
            pltpu.VMEM((rows, D_MODEL), BF16),
            pltpu.VMEM((rows, D_MODEL), F32),
        ],
        compiler_params=pltpu.CompilerParams(dimension_semantics=("arbitrary",)),
        name="ffn_sample",
    )(x, att, cvn, gate1, shift2, scale2, gate2, att_g, norm2_g, wo_bf, w1_bf, w1_bf, cw, cb, w2_bf, pm1, pm2)


def _page_mean_kernel(pt_ref, *refs):
    del pt_ref
    page_refs, o_ref = refs[:-1], refs[-1]
    ppb = MOBA_BLOCK // PAGE_SIZE
    rows = []
    for n in range(len(page_refs) // ppb):
        s = jnp.sum(page_refs[ppb * n][0, 0], axis=0, keepdims=True)
        for p in range(1, ppb):
            s = s + jnp.sum(page_refs[ppb * n + p][0, 0], axis=0, keepdims=True)
        rows.append(s * (1.0 / MOBA_BLOCK))
    o_ref[0, 0] = jnp.concatenate(rows, axis=0)


def _page_means(cache_k4, pt_flat, *, nb, n_pages):
    depth = cache_k4.shape[0]
    ppb = MOBA_BLOCK // PAGE_SIZE
    steps = n_pages // PAGES_PER_STEP
    bps = PAGES_PER_STEP // ppb

    def page_spec(p):
        return pl.BlockSpec((1, 1, PAGE_SIZE, ATT_WIDTH),
                            lambda l, b, s, pt: (l, pt[b * n_pages + s * PAGES_PER_STEP + p], 0, 0))

    grid_spec = pltpu.PrefetchScalarGridSpec(
        num_scalar_prefetch=1,
        grid=(depth, nb, steps),
        in_specs=[page_spec(p) for p in range(PAGES_PER_STEP)],
        out_specs=pl.BlockSpec((1, 1, bps, ATT_WIDTH), lambda l, b, s, pt: (l, b, s, 0)),
    )
    return pl.pallas_call(
        _page_mean_kernel,
        grid_spec=grid_spec,
        out_shape=jax.ShapeDtypeStruct((depth, nb, n_pages // ppb, ATT_WIDTH), F32),
        compiler_params=pltpu.CompilerParams(dimension_semantics=("arbitrary", "arbitrary", "arbitrary")),
        name="page_means",
    )(pt_flat, *([cache_k4] * PAGES_PER_STEP))


def _topk_sample_kernel(q_ref, km_ref, o_ref):
    q = q_ref[0]
    km = km_ref[0]
    rows, nblk = q.shape[0], km.shape[0]
    lane = lax.broadcasted_iota(jnp.int32, (rows, ATT_WIDTH), 1)
    blane = lax.broadcasted_iota(jnp.int32, (rows, nblk), 1).astype(F32)
    olane = lax.broadcasted_iota(jnp.int32, (rows, LANES), 1)
    for hd in range(ATT_HEADS):
        qz = jnp.where((lane >= hd * HEAD_DIM) & (lane < (hd + 1) * HEAD_DIM), q, 0.0)
        gs = _dot3_nt(qz, km)
        _, firsts = _top3_select(gs, blane)
        out = jnp.zeros((rows, LANES), jnp.int32)
        for t, f in enumerate(firsts):
            out = jnp.where(olane == t, f.astype(jnp.int32), out)
        o_ref[0, hd] = out


def _topk_sample(q8, kmean):
    nb, rows, _ = q8.shape
    nblk = kmean.shape[1]
    return pl.pallas_call(
        _topk_sample_kernel,
        grid=(nb,),
        in_specs=[
            pl.BlockSpec((1, rows, ATT_WIDTH), lambda b: (b, 0, 0)),
            pl.BlockSpec((1, nblk, ATT_WIDTH), lambda b: (b, 0, 0)),
        ],
        out_specs=pl.BlockSpec((1, ATT_HEADS, rows, LANES), lambda b: (b, 0, 0, 0)),
        out_shape=jax.ShapeDtypeStruct((nb, ATT_HEADS, rows, LANES), jnp.int32),
        compiler_params=pltpu.CompilerParams(dimension_semantics=("arbitrary",)),
        name="topk_sample",
    )(q8, kmean)


def _attn_sample_kernel(idx_ref, pt_ref, q_ref, kn_ref, vn_ref, *refs, dec_seq, n_sel_pages):
    del idx_ref, pt_ref
    kp_refs = refs[:n_sel_pages]
    vp_refs = refs[n_sel_pages:2 * n_sel_pages]
    o_ref = refs[2 * n_sel_pages]
    hd = pl.program_id(1)
    odd = hd % 2
    q = q_ref[0]
    rows = q.shape[0]
    lane = lax.broadcasted_iota(jnp.int32, (rows, LANES), 1)
    inhead = (lane >= odd * HEAD_DIM) & (lane < (odd + 1) * HEAD_DIM)
    qz = jnp.where(inhead, q * (HEAD_DIM ** -0.5), 0.0).astype(BF16)
    prow = lax.broadcasted_iota(jnp.int32, (rows, PAGE_SIZE), 0)
    pages_per_query = n_sel_pages // dec_seq

    ss = []
    for n in range(n_sel_pages):
        s = _dot_nt(qz, kp_refs[n][0, 0].astype(BF16))
        ss.append(jnp.where(prow == n // pages_per_query, s, -jnp.inf))
    orow = lax.broadcasted_iota(jnp.int32, (rows, rows), 0)
    ocol = lax.broadcasted_iota(jnp.int32, (rows, rows), 1)
    s_own = _dot_nt(qz, kn_ref[0].astype(BF16))
    s_own = jnp.where((ocol <= orow) & (ocol < dec_seq), s_own, -jnp.inf)

    m = jnp.max(s_own, axis=1, keepdims=True)
    for s in ss:
        m = jnp.maximum(m, jnp.max(s, axis=1, keepdims=True))
    p_own = jnp.exp(s_own - m)
    den = jnp.sum(p_own, axis=1, keepdims=True)
    acc = _dot(p_own.astype(BF16), vn_ref[0].astype(BF16))
    for n in range(n_sel_pages):
        p = jnp.exp(ss[n] - m)
        den = den + jnp.sum(p, axis=1, keepdims=True)
        acc = acc + _dot(p.astype(BF16), vp_refs[n][0, 0].astype(BF16))
    val = acc / den

    @pl.when(odd == 0)
    def _():
        o_ref[0] = val

    @pl.when(odd == 1)
    def _():
        o_ref[0] = jnp.where(lane < HEAD_DIM, o_ref[0], val)


def _attention_sample(idx_flat, pt_flat, q8, kn8, vn8, cache_k4, cache_v4, *, layer, nb, n_pages, dec_seq):
    ppb = MOBA_BLOCK // PAGE_SIZE
    rows = q8.shape[1]
    n_sel_pages = dec_seq * MOBA_TOPK * ppb

    def page_spec(n):
        qi, t, p = n // (MOBA_TOPK * ppb), (n // ppb) % MOBA_TOPK, n % ppb

        def imap(b, h, idx, pt):
            blk = idx[((b * ATT_HEADS + h) * dec_seq + qi) * MOBA_TOPK + t]
            return (layer, pt[b * n_pages + blk * ppb + p], 0, h // 2)

        return pl.BlockSpec((1, 1, PAGE_SIZE, LANES), imap)

    pair = pl.BlockSpec((1, rows, LANES), lambda b, h, idx, pt: (b, 0, h // 2))
    grid_spec = pltpu.PrefetchScalarGridSpec(
        num_scalar_prefetch=2,
        grid=(nb, ATT_HEADS),
        in_specs=[pair, pair, pair] + [page_spec(n) for n in range(n_sel_pages)] * 2,
        out_specs=pl.BlockSpec((1, rows, LANES), lambda b, h, idx, pt: (b, 0, h // 2)),
    )
    return pl.pallas_call(
        functools.partial(_attn_sample_kernel, dec_seq=dec_seq, n_sel_pages=n_sel_pages),
        grid_spec=grid_spec,
        out_shape=jax.ShapeDtypeStruct((nb, rows, ATT_WIDTH), F32),
        compiler_params=pltpu.CompilerParams(dimension_semantics=("arbitrary", "arbitrary")),
        name="attention_sample",
    )(idx_flat, pt_flat, q8, kn8, vn8, *([cache_k4] * n_sel_pages), *([cache_v4] * n_sel_pages))


def _rope_tables(pos):
    half = ROT_DIM // 2
    inv = jnp.exp(-math.log(ROPE_THETA) * jnp.arange(half, dtype=F32) * (2.0 / ROT_DIM))
    ang = pos.astype(F32)[:, None] * inv[None, :]
    cos, sin = jnp.cos(ang), jnp.sin(ang)
    n = pos.shape[0]
    pad = jnp.zeros((n, HEAD_DIM - ROT_DIM), F32)
    zero = jnp.zeros((n, half), F32)
    cos_h = jnp.concatenate([cos, cos, pad + 1.0], axis=1)
    s1_h = jnp.concatenate([-sin, zero, pad], axis=1)
    s2_h = jnp.concatenate([zero, sin, pad], axis=1)
    two = lambda t: jnp.concatenate([t, t], axis=1)
    return two(cos_h), two(s1_h), two(s2_h)


def _pad_rows(a, rows):
    return jnp.pad(a, ((0, 0), (0, rows - a.shape[1]), (0, 0)))


def _sample_attend(qs, ks, vs, kmean_l, pt_flat, cache_k4, cache_v4, *, layer, nbs, dec_seq, n_pages):
    q8 = _pad_rows(qs.reshape(nbs, dec_seq, ATT_WIDTH), 8)
    k8 = _pad_rows(ks.reshape(nbs, dec_seq, ATT_WIDTH), 8)
    v8 = _pad_rows(vs.reshape(nbs, dec_seq, ATT_WIDTH), 8)
    idx = _topk_sample(q8, kmean_l)
    idx_flat = idx[:, :, :dec_seq, :MOBA_TOPK].reshape(-1)
    o = _attention_sample(idx_flat, pt_flat, q8, k8, v8, cache_k4, cache_v4,
                          layer=layer, nb=nbs, n_pages=n_pages, dec_seq=dec_seq)
    return o[:, :dec_seq].reshape(nbs * dec_seq, ATT_WIDTH)


def kernel(x_prompt, x_sample, cache_k, cache_v, state_conv, state_ffn, page_table, c_prompt, c_sample,
           norm1_g, norm2_g, w_ada, b_ada, w_in, q_norm_g, k_norm_g, conv_w, conv_b, conv_ln_g, conv_ln_b,
           attn_out_g, conv_out_g, w_out, w_fc1, ffn_conv_w, ffn_conv_b, w_fc2):
    nb, seq, _ = x_prompt.shape
    nbs, dec_seq, _ = x_sample.shape
    depth = w_in.shape[0]
    n_pages = page_table.shape[1]
    past = n_pages * PAGE_SIZE
    n_pool = cache_k.shape[1]
    n_rows_s = nbs * dec_seq
    assert past % MOBA_BLOCK == 0 and past // MOBA_BLOCK >= MOBA_TOPK
    assert seq % TM_IN == 0 and seq % TM_FFN == 0 and TM_IN % MOBA_BLOCK == 0 and n_pages % PAGES_PER_STEP == 0

    n_c = nb + nbs
    c_rows = -(-n_c // 8) * 8
    c_all = jnp.pad(jnp.concatenate([c_prompt, c_sample], axis=0), ((0, c_rows - n_c), (0, 0)))
    m_all = _modulation(c_all, w_ada, b_ada)

    w_in_bf, w_out_bf = w_in.astype(BF16), w_out.astype(BF16)
    w_fc1_bf, w_fc2_bf = w_fc1.astype(BF16), w_fc2.astype(BF16)
    head_id = jnp.arange(ATT_WIDTH) // HEAD_DIM
    bd = jnp.where(head_id[:, None] == head_id[None, :], 1.0 / HEAD_DIM, 0.0).astype(BF16)

    tabs_p = _rope_tables(jnp.arange(seq))
    tabs_s = _rope_tables(past + (jnp.arange(n_rows_s) % dec_seq))

    cache_k4 = cache_k.reshape(depth, n_pool, PAGE_SIZE, ATT_WIDTH)
    cache_v4 = cache_v.reshape(depth, n_pool, PAGE_SIZE, ATT_WIDTH)
    pt_flat = page_table.reshape(-1)
    kmean_s = _page_means(cache_k4, pt_flat, nb=nbs, n_pages=n_pages)

    xp = x_prompt.reshape(nb * seq, D_MODEL)
    xs = x_sample.reshape(n_rows_s, D_MODEL)
    outs = [[] for _ in range(8)]
    for l in range(depth):
        row = lambda a: a[l].reshape(1, -1)
        mp = m_all[l, :nb].reshape(nb, N_MOD, 1, D_MODEL)
        ms = jnp.repeat(m_all[l, nb:n_c].reshape(nbs, N_MOD, D_MODEL), dec_seq, axis=0)
        mp = [mp[:, i] for i in range(N_MOD)]
        ms = [ms[:, i] for i in range(N_MOD)]
        qg = jnp.tile(q_norm_g[l], ATT_HEADS).reshape(1, -1)
        kg = jnp.tile(k_norm_g[l], ATT_HEADS).reshape(1, -1)

        q, k, v, u, kaug, vaug, kmean = _mixer_in(
            xp, mp[0], mp[1], row(norm1_g), w_in_bf[l], bd, qg, kg, *tabs_p, prompt=True, tm=TM_IN, seq=seq)
        nblk = seq // MOBA_BLOCK
        km = kmean.reshape(nb, nblk, ATT_HEADS, HEAD_DIM).transpose(0, 2, 1, 3)
        kmw = jnp.pad(km, ((0, 0), (0, 0), (HEAD_DIM, LANES - HEAD_DIM - nblk), (0, LANES - HEAD_DIM)))
        att = _attention_prompt(q, kaug, vaug, kmw, nb=nb, seq=seq)
        cvn = _conv_prompt(u, conv_w[l], row(conv_b), row(conv_ln_g), row(conv_ln_b), row(conv_out_g),
                           nb=nb, seq=seq)
        xp_new, ffn_state_p = _ffn_prompt(
            xp, att.reshape(nb * seq, ATT_WIDTH), cvn.reshape(nb * seq, CONV_WIDTH),
            mp[2], mp[3], mp[4], mp[5], row(attn_out_g), row(norm2_g),
            w_out_bf[l], w_fc1_bf[l], ffn_conv_w[l], row(ffn_conv_b), w_fc2_bf[l], nb=nb, seq=seq)
        outs[0].append(k.reshape(nb, seq, ATT_HEADS, HEAD_DIM))
        outs[1].append(v.reshape(nb, seq, ATT_HEADS, HEAD_DIM))
        outs[2].append(u.reshape(nb, seq, CONV_WIDTH)[:, seq - (CONV_K - 1):])
        outs[3].append(ffn_state_p)
        xp = xp_new

        qs, ks, vs, us = _mixer_in(
            xs, ms[0][None], ms[1][None], row(norm1_g), w_in_bf[l], bd, qg, kg, *tabs_s,
            prompt=False, tm=n_rows_s, seq=n_rows_s)
        att_s = _sample_attend(qs, ks, vs, kmean_s[l], pt_flat, cache_k4, cache_v4,
                               layer=l, nbs=nbs, dec_seq=dec_seq, n_pages=n_pages)
        u_ext = jnp.concatenate([state_conv[l], us.reshape(nbs, dec_seq, CONV_WIDTH)], axis=1)
        cvn_s = _conv_sample(u_ext, conv_w[l], row(conv_b), row(conv_ln_g), row(conv_ln_b), row(conv_out_g))
        st = state_ffn[l]
        pm1 = jnp.repeat(st[:, 1], dec_seq, axis=0)
        pm2 = jnp.stack([st[:, 0], st[:, 1]] + [st[:, 1]] * (dec_seq - 2), axis=1).reshape(n_rows_s, D_FF)
        xs_new, g_s = _ffn_sample(
            xs, att_s, cvn_s.reshape(n_rows_s, CONV_WIDTH), ms[2], ms[3], ms[4], ms[5],
            row(attn_out_g), row(norm2_g), w_out_bf[l], w_fc1_bf[l], ffn_conv_w[l], row(ffn_conv_b),
            w_fc2_bf[l], pm1, pm2, seq=dec_seq)
        outs[4].append(ks.reshape(nbs, dec_seq, ATT_HEADS, HEAD_DIM))
        outs[5].append(vs.reshape(nbs, dec_seq, ATT_HEADS, HEAD_DIM))
        outs[6].append(u_ext[:, dec_seq:])
        outs[7].append(g_s.reshape(nbs, dec_seq, D_FF)[:, dec_seq - (FFN_CONV_K - 1):])
        xs = xs_new

    stk = [jnp.stack(o) for o in outs]
    return (xp.reshape(nb, seq, D_MODEL), xs.reshape(nbs, dec_seq, D_MODEL),
            stk[0], stk[1], stk[2], stk[3], stk[4], stk[5], stk[6], stk[7])
```

```python
import functools
import math

import jax
import jax.numpy as jnp
from jax import lax
from jax.experimental import pallas as pl
from jax.experimental.pallas import tpu as pltpu

F32 = jnp.float32
BF16 = jnp.bfloat16

D_MODEL = 1024
HEAD_DIM = 64
ATT_HEADS = 8
ATT_WIDTH = 512
CONV_WIDTH = 512
CONV_K = 31
ROT_DIM = 16
ROPE_THETA = 500000.0
MOBA_BLOCK = 256
MOBA_TOPK = 3
PAGE_SIZE = 128
D_FF = 2816
FFN_CONV_K = 3
N_MOD = 6
EPS = 1e-6
IN_WIDTH = 3 * ATT_WIDTH + 2 * CONV_WIDTH

LANES = 128
NEG_BIG = -1e30
FF_CHUNK = 256
N_FF_CHUNKS = D_FF // FF_CHUNK
PAGES_PER_STEP = 32

TM_IN = 512
TM_FFN = 512
TC_CONV = 256
CONV_HALO = 32
CONV_ROWS = 64
ATT_KEY_BLOCKS = 4


def _dot(a, b):
    return jnp.dot(a, b, preferred_element_type=F32)


def _dot_nt(a, b):
    return lax.dot_general(a, b, (((1,), (1,)), ((), ())), preferred_element_type=F32)


def _split(a):
    hi = a.astype(BF16)
    lo = (a - hi.astype(F32)).astype(BF16)
    return hi, lo


def _dot3(a, b):
    ah, al = _split(a)
    bh, bl = _split(b)
    return _dot(ah, bh) + _dot(al, bh) + _dot(ah, bl)


def _dot3_nt(a, b):
    ah, al = _split(a)
    bh, bl = _split(b)
    return _dot_nt(ah, bh) + _dot_nt(al, bh) + _dot_nt(ah, bl)


def _sigmoid(x):
    return 1.0 / (1.0 + jnp.exp(-x))


def _silu(x):
    return x * _sigmoid(x)


def _rms(x, g):
    ms = jnp.mean(x * x, axis=-1, keepdims=True)
    return x * lax.rsqrt(ms + EPS) * g


def _mod_kernel(c_ref, w_ref, b_ref, o_ref):
    a = _silu(c_ref[...])
    o_ref[0] = _dot3(a, w_ref[0]) + b_ref[0]


def _modulation(c_all, w_ada, b_ada):
    depth = w_ada.shape[0]
    rows = c_all.shape[0]
    nt = (N_MOD * D_MODEL) // D_MODEL
    return pl.pallas_call(
        _mod_kernel,
        grid=(depth, nt),
        in_specs=[
            pl.BlockSpec((rows, D_MODEL), lambda l, n: (0, 0)),
            pl.BlockSpec((1, D_MODEL, D_MODEL), lambda l, n: (l, 0, n)),
            pl.BlockSpec((1, 1, D_MODEL), lambda l, n: (l, 0, n)),
        ],
        out_specs=pl.BlockSpec((1, rows, D_MODEL), lambda l, n: (l, 0, n)),
        out_shape=jax.ShapeDtypeStruct((depth, rows, N_MOD * D_MODEL), F32),
        name="modulation",
    )(c_all, w_ada, b_ada.reshape(depth, 1, N_MOD * D_MODEL))


def _tile_lanes(t, n):
    return jnp.concatenate([t] * n, axis=1)


def _mixer_in_kernel(x_ref, sh_ref, sc_ref, ng_ref, w_ref, bd_ref, qg_ref, kg_ref,
                     cos_ref, s1_ref, s2_ref, q_ref, k_ref, v_ref, u_ref, *extra,
                     prompt, tm, tiles_per_seq):
    x = x_ref[...]
    h = _rms(x, ng_ref[...])
    h = h * (1.0 + sc_ref[0]) + sh_ref[0]
    z = _dot(h.astype(BF16), w_ref[...])

    reps = ATT_WIDTH // LANES
    cos = _tile_lanes(cos_ref[...], reps)
    s1 = _tile_lanes(s1_ref[...], reps)
    s2 = _tile_lanes(s2_ref[...], reps)
    bd = bd_ref[...]

    def head_norm_rope(t, g):
        hi, lo = _split(t * t)
        ms = _dot(hi, bd) + _dot(lo, bd)
        tn = t * lax.rsqrt(ms + EPS) * g
        return (tn * cos + pltpu.roll(tn, ATT_WIDTH - ROT_DIM // 2, 1) * s1
                + pltpu.roll(tn, ROT_DIM // 2, 1) * s2)

    q = head_norm_rope(z[:, 0:ATT_WIDTH], qg_ref[...])
    k = head_norm_rope(z[:, ATT_WIDTH:2 * ATT_WIDTH], kg_ref[...])
    v = z[:, 2 * ATT_WIDTH:3 * ATT_WIDTH]
    a = z[:, 3 * ATT_WIDTH:3 * ATT_WIDTH + CONV_WIDTH]
    g = z[:, 3 * ATT_WIDTH + CONV_WIDTH:]
    q_ref[...] = q
    k_ref[...] = k
    v_ref[...] = v
    u_ref[...] = a * _sigmoid(g)

    if prompt:
        ka_ref, va_ref, km_ref = extra
        t_in_seq = pl.program_id(0) % tiles_per_seq
        lane = lax.broadcasted_iota(jnp.int32, (tm, LANES), 1)
        row = lax.broadcasted_iota(jnp.int32, (tm, LANES), 0)
        blk = (t_in_seq * tm + row) // MOBA_BLOCK
        onehot = jnp.where(lane - HEAD_DIM == blk, 1.0, 0.0)
        for hd in range(ATT_HEADS):
            lo_l = LANES * (hd // 2)
            kt = k[:, lo_l:lo_l + LANES]
            vt = v[:, lo_l:lo_l + LANES]
            if hd % 2 == 1:
                kt = pltpu.roll(kt, HEAD_DIM, 1)
                vt = pltpu.roll(vt, HEAD_DIM, 1)
            ka_ref[0, hd] = jnp.where(lane < HEAD_DIM, kt, onehot).astype(BF16)
            va_ref[0, hd] = jnp.where(lane < HEAD_DIM, vt, 1.0).astype(BF16)
        for c in range(tm // MOBA_BLOCK):
            km_ref[c] = jnp.sum(k[c * MOBA_BLOCK:(c + 1) * MOBA_BLOCK], axis=0, keepdims=True) * (1.0 / MOBA_BLOCK)
    else:
        lane = lax.broadcasted_iota(jnp.int32, (tm, LANES), 1)
        for src, dst in zip((q, k, v), extra):
            for hd in range(ATT_HEADS):
                lo_l = LANES * (hd // 2)
                t = src[:, lo_l:lo_l + LANES]
                if hd % 2 == 1:
                    t = pltpu.roll(t, HEAD_DIM, 1)
                dst[hd] = jnp.where(lane < HEAD_DIM, t, 0.0)


def _mixer_in(x, shift, scale, norm_g, w_in_bf, bd, qg, kg, cos_t, s1_t, s2_t, *, prompt, tm, seq):
    n = x.shape[0]
    nt = n // tm
    tiles_per_seq = seq // tm
    mod_rows = shift.shape[1]
    if prompt:
        mod_map = lambda t: (t // tiles_per_seq, 0, 0)
        tab_map = lambda t: (t % tiles_per_seq, 0)
    else:
        mod_map = lambda t: (t, 0, 0)
        tab_map = lambda t: (t, 0)
    const = lambda t: (0, 0)
    row_map = lambda t: (t, 0)
    in_specs = [
        pl.BlockSpec((tm, D_MODEL), row_map),
        pl.BlockSpec((1, mod_rows, D_MODEL), mod_map),
        pl.BlockSpec((1, mod_rows, D_MODEL), mod_map),
        pl.BlockSpec((1, D_MODEL), const),
        pl.BlockSpec((D_MODEL, IN_WIDTH), const),
        pl.BlockSpec((ATT_WIDTH, ATT_WIDTH), const),
        pl.BlockSpec((1, ATT_WIDTH), const),
        pl.BlockSpec((1, ATT_WIDTH), const),
        pl.BlockSpec((tm, LANES), tab_map),
        pl.BlockSpec((tm, LANES), tab_map),
        pl.BlockSpec((tm, LANES), tab_map),
    ]
    out_specs = [pl.BlockSpec((tm, ATT_WIDTH), row_map)] * 4
    out_shape = [jax.ShapeDtypeStruct((n, ATT_WIDTH), F32)] * 4
    if prompt:
        nb = n // seq
        aug_map = lambda t: (t // tiles_per_seq, 0, t % tiles_per_seq, 0)
        out_specs += [pl.BlockSpec((1, ATT_HEADS, tm, LANES), aug_map)] * 2
        out_shape += [jax.ShapeDtypeStruct((nb, ATT_HEADS, seq, LANES), BF16)] * 2
        bpt = tm // MOBA_BLOCK
        out_specs += [pl.BlockSpec((bpt, 1, ATT_WIDTH), lambda t: (t, 0, 0))]
        out_shape += [jax.ShapeDtypeStruct((n // MOBA_BLOCK, 1, ATT_WIDTH), F32)]
    else:
        out_specs += [pl.BlockSpec((ATT_HEADS, tm, LANES), lambda t: (0, t, 0))] * 3
        out_shape += [jax.ShapeDtypeStruct((ATT_HEADS, n, LANES), F32)] * 3
    return pl.pallas_call(
        functools.partial(_mixer_in_kernel, prompt=prompt, tm=tm, tiles_per_seq=tiles_per_seq),
        grid=(nt,),
        in_specs=in_specs,
        out_specs=out_specs,
        out_shape=out_shape,
        compiler_params=pltpu.CompilerParams(dimension_semantics=("arbitrary",)),
        name="mixer_in_prompt" if prompt else "mixer_in_sample",
    )(x, shift, scale, norm_g, w_in_bf, bd, qg, kg, cos_t, s1_t, s2_t)


def _top3_select(cur, lanef, axis=1):
    selm = jnp.zeros(cur.shape, F32)
    firsts = []
    for _ in range(MOBA_TOPK):
        m = jnp.max(cur, axis=axis, keepdims=True)
        first = jnp.min(jnp.where(cur == m, lanef, 1e9), axis=axis, keepdims=True)
        pick = (lanef == first) & (m > -jnp.inf)
        selm = jnp.where(pick, 1.0, selm)
        cur = jnp.where(pick, -jnp.inf, cur)
        firsts.append(first)
    return selm, firsts


def _attn_kernel(q_ref, ka_ref, va_ref, kmw_ref, o_ref, *, kb):
    i = pl.program_id(2)
    tq = MOBA_BLOCK
    q2 = q_ref[0]
    lane = lax.broadcasted_iota(jnp.int32, (tq, LANES), 1)
    nblk = kmw_ref.shape[2]
    blk = lax.broadcasted_iota(jnp.int32, (nblk, tq), 0)
    blkf = blk.astype(F32)
    past = blk < i
    place = jnp.where(lax.broadcasted_iota(jnp.int32, (nblk, LANES), 1)
                      == lax.broadcasted_iota(jnp.int32, (nblk, LANES), 0) + HEAD_DIM, 1.0, 0.0).astype(BF16)
    gw = kb * tq
    g_own = i // kb
    row = lax.broadcasted_iota(jnp.int32, (tq, gw), 0)
    col = lax.broadcasted_iota(jnp.int32, (tq, gw), 1)
    causal = col + g_own * gw <= row + i * tq

    qas = []
    for hh in range(2):
        qh = q2 if hh == 0 else pltpu.roll(q2, HEAD_DIM, 1)
        qz = jnp.where(lane < HEAD_DIM, qh, 0.0)
        gs = _dot3_nt(kmw_ref[0, hh], qz)
        selm, _ = _top3_select(jnp.where(past, gs, -jnp.inf), blkf, axis=0)
        bias_t = jnp.where(past & (selm == 0.0), NEG_BIG, 0.0).astype(BF16)
        bias = lax.dot_general(bias_t, place, (((0,), (0,)), ((), ())), preferred_element_type=F32)
        qas.append(jnp.where(lane < HEAD_DIM, qh * (HEAD_DIM ** -0.5), bias).astype(BF16))

    def group_rows(g):
        return pl.ds(pl.multiple_of(g * gw, gw), gw)

    init = []
    for hh in range(2):
        s = _dot_nt(qas[hh], ka_ref[0, hh, group_rows(g_own), :])
        s = jnp.where(causal, s, -jnp.inf)
        m = jnp.max(s, axis=1, keepdims=True)
        p = jnp.exp(s - m)
        acc = _dot(p.astype(BF16), va_ref[0, hh, group_rows(g_own), :])
        init.append((m, acc))

    def body(g, carry):
        new = []
        for hh in range(2):
            m, acc = carry[hh]
            s = _dot_nt(qas[hh], ka_ref[0, hh, group_rows(g), :])
            mn = jnp.maximum(m, jnp.max(s, axis=1, keepdims=True))
            p = jnp.exp(s - mn)
            acc = acc * jnp.exp(m - mn) + _dot(p.astype(BF16), va_ref[0, hh, group_rows(g), :])
            new.append((mn, acc))
        return tuple(new)

    (_, acc0), (_, acc1) = lax.fori_loop(0, g_own, body, tuple(init))
    o0 = acc0 / pltpu.roll(acc0, HEAD_DIM, 1)
    o1 = pltpu.roll(acc1, HEAD_DIM, 1) / acc1
    o_ref[0] = jnp.where(lane < HEAD_DIM, o0, o1)


def _attention_prompt(q, kaug, vaug, kmw, *, nb, seq):
    nq = seq // MOBA_BLOCK
    hp = ATT_HEADS // 2
    kb = math.gcd(ATT_KEY_BLOCKS, nq)
    return pl.pallas_call(
        functools.partial(_attn_kernel, kb=kb),
        grid=(nb, hp, nq),
        in_specs=[
            pl.BlockSpec((1, MOBA_BLOCK, LANES), lambda b, h, i: (b, i, h)),
            pl.BlockSpec((1, 2, seq, LANES), lambda b, h, i: (b, h, 0, 0)),
            pl.BlockSpec((1, 2, seq, LANES), lambda b, h, i: (b, h, 0, 0)),
            pl.BlockSpec((1, 2, nq, LANES), lambda b, h, i: (b, h, 0, 0)),
        ],
        out_specs=pl.BlockSpec((1, MOBA_BLOCK, LANES), lambda b, h, i: (b, i, h)),
        out_shape=jax.ShapeDtypeStruct((nb, seq, ATT_WIDTH), F32),
        compiler_params=pltpu.CompilerParams(dimension_semantics=("arbitrary", "arbitrary", "arbitrary")),
        name="attention_prompt",
    )(q.reshape(nb, seq, ATT_WIDTH), kaug, vaug, kmw)


def _conv_post(y, lg, lb, cg):
    mu = jnp.mean(y, axis=-1, keepdims=True)
    yc = y - mu
    var = jnp.mean(yc * yc, axis=-1, keepdims=True)
    cv = _silu(yc * lax.rsqrt(var + EPS) * lg + lb)
    return _rms(cv, cg)


def _conv_kernel(prev_ref, cur_ref, w_ref, b_ref, lg_ref, lb_ref, cg_ref, o_ref, ext_ref, *, tc):
    t = pl.program_id(1)
    ext_ref[0:CONV_HALO, :] = jnp.where(t > 0, prev_ref[0], 0.0)
    ext_ref[CONV_HALO:CONV_HALO + tc, :] = cur_ref[0]
    off = CONV_HALO - (CONV_K - 1)
    for c in range(tc // CONV_ROWS):
        acc = jnp.zeros((CONV_ROWS, CONV_WIDTH), F32)
        for j in range(CONV_K):
            acc = acc + w_ref[j:j + 1, :] * ext_ref[pl.ds(c * CONV_ROWS + off + j, CONV_ROWS), :]
        y = acc + b_ref[...]
        o_ref[0, c * CONV_ROWS:(c + 1) * CONV_ROWS, :] = _conv_post(
            y, lg_ref[...], lb_ref[...], cg_ref[...]).astype(o_ref.dtype)


def _conv_prompt(u, conv_w, conv_b, ln_g, ln_b, cg, *, nb, seq):
    tc = TC_CONV
    nt = seq // tc
    hpt = tc // CONV_HALO
    vec = pl.BlockSpec((1, CONV_WIDTH), lambda b, t: (0, 0))
    u3 = u.reshape(nb, seq, CONV_WIDTH)
    return pl.pallas_call(
        functools.partial(_conv_kernel, tc=tc),
        grid=(nb, nt),
        in_specs=[
            pl.BlockSpec((1, CONV_HALO, CONV_WIDTH), lambda b, t: (b, jnp.maximum(t * hpt - 1, 0), 0)),
            pl.BlockSpec((1, tc, CONV_WIDTH), lambda b, t: (b, t, 0)),
            pl.BlockSpec((CONV_K, CONV_WIDTH), lambda b, t: (0, 0)),
            vec, vec, vec, vec,
        ],
        out_specs=pl.BlockSpec((1, tc, CONV_WIDTH), lambda b, t: (b, t, 0)),
        out_shape=jax.ShapeDtypeStruct((nb, seq, CONV_WIDTH), BF16),
        scratch_shapes=[pltpu.VMEM((CONV_HALO + tc, CONV_WIDTH), F32)],
        compiler_params=pltpu.CompilerParams(dimension_semantics=("arbitrary", "arbitrary")),
        name="conv_prompt",
    )(u3, u3, conv_w, conv_b, ln_g, ln_b, cg)


def _conv_sample_kernel(ue_ref, w_ref, b_ref, lg_ref, lb_ref, cg_ref, o_ref, *, rows):
    ue = ue_ref[0]
    w = w_ref[...]
    ys = [jnp.sum(ue[t:t + CONV_K, :] * w, axis=0, keepdims=True) for t in range(rows)]
    y = jnp.concatenate(ys, axis=0) + b_ref[...]
    o_ref[0] = _conv_post(y, lg_ref[...], lb_ref[...], cg_ref[...]).astype(o_ref.dtype)


def _conv_sample(u_ext, conv_w, conv_b, ln_g, ln_b, cg):
    nb, ext_rows, _ = u_ext.shape
    rows = ext_rows - (CONV_K - 1)
    vec = pl.BlockSpec((1, CONV_WIDTH), lambda b: (0, 0))
    return pl.pallas_call(
        functools.partial(_conv_sample_kernel, rows=rows),
        grid=(nb,),
        in_specs=[
            pl.BlockSpec((1, ext_rows, CONV_WIDTH), lambda b: (b, 0, 0)),
            pl.BlockSpec((CONV_K, CONV_WIDTH), lambda b: (0, 0)),
            vec, vec, vec, vec,
        ],
        out_specs=pl.BlockSpec((1, rows, CONV_WIDTH), lambda b: (b, 0, 0)),
        out_shape=jax.ShapeDtypeStruct((nb, rows, CONV_WIDTH), F32),
        compiler_params=pltpu.CompilerParams(dimension_semantics=("arbitrary",)),
        name="conv_sample",
    )(u_ext, conv_w, conv_b, ln_g, ln_b, cg)


def _merge(x, att, cvn, g1, ag, wo_ref):
    attn = _rms(att, ag).astype(BF16)
    merged = _dot(attn, wo_ref[0:ATT_WIDTH, :]) + _dot(cvn.astype(BF16), wo_ref[ATT_WIDTH:, :])
    return x + g1 * merged


def _ffn_prompt_kernel(x_ref, att_ref, cvn_ref, g1_ref, sh_ref, sc_ref, g2_ref, ag_ref, ng_ref,
                       wo_ref, w1_ref, cw_ref, cb_ref, w2_ref, o_ref, st_ref, carry_ref, act_ref,
                       *, tm, tiles_per_seq):
    t_in_seq = pl.program_id(0) % tiles_per_seq

    @pl.when(t_in_seq == 0)
    def _():
        carry_ref[...] = jnp.zeros(carry_ref.shape, F32)

    x1 = _merge(x_ref[...], att_ref[...], cvn_ref[...], g1_ref[0], ag_ref[...], wo_ref)
    h2 = (_rms(x1, ng_ref[...]) * (1.0 + sc_ref[0]) + sh_ref[0]).astype(BF16)
    row = lax.broadcasted_iota(jnp.int32, (tm, FF_CHUNK), 0)
    for c in range(N_FF_CHUNKS):
        cols = slice(c * FF_CHUNK, (c + 1) * FF_CHUNK)
        g = _dot(h2, w1_ref[:, cols])
        up = _dot(h2, w1_ref[:, D_FF + c * FF_CHUNK:D_FF + (c + 1) * FF_CHUNK])
        p0 = carry_ref[0:1, cols]
        p1 = carry_ref[1:2, cols]
        gm1 = jnp.where(row == 0, p1, pltpu.roll(g, 1, 0))
        gm2 = jnp.where(row == 0, p0, jnp.where(row == 1, p1, pltpu.roll(g, 2, 0)))
        y = cw_ref[0:1, cols] * gm2 + cw_ref[1:2, cols] * gm1 + cw_ref[2:3, cols] * g + cb_ref[:, cols]
        act_ref[:, cols] = (_silu(y) * up).astype(BF16)
        tail = g[tm - (FFN_CONV_K - 1):tm, :]
        carry_ref[0:FFN_CONV_K - 1, cols] = tail
        st_ref[0, :, cols] = tail
    o_ref[...] = x1 + g2_ref[0] * _dot(act_ref[...], w2_ref[...])


def _ffn_prompt(x, att, cvn, gate1, shift2, scale2, gate2, att_g, norm2_g, wo_bf, w1_bf, cw, cb, w2_bf,
                *, nb, seq):
    tm = TM_FFN
    n = x.shape[0]
    nt = n // tm
    tiles_per_seq = seq // tm
    row_map = lambda t: (t, 0)
    mod_map = lambda t: (t // tiles_per_seq, 0, 0)
    const = lambda t: (0, 0)
    mod = pl.BlockSpec((1, 1, D_MODEL), mod_map)

    def resident(shape):
        return pl.BlockSpec(shape, const, pipeline_mode=pl.Buffered(1))

    return pl.pallas_call(
        functools.partial(_ffn_prompt_kernel, tm=tm, tiles_per_seq=tiles_per_seq),
        grid=(nt,),
        in_specs=[
            pl.BlockSpec((tm, D_MODEL), row_map),
            pl.BlockSpec((tm, ATT_WIDTH), row_map),
            pl.BlockSpec((tm, CONV_WIDTH), row_map),
            mod, mod, mod, mod,
            pl.BlockSpec((1, ATT_WIDTH), const),
            pl.BlockSpec((1, D_MODEL), const),
            resident((D_MODEL, D_MODEL)),
            resident((D_MODEL, 2 * D_FF)),
            pl.BlockSpec((FFN_CONV_K, D_FF), const),
            pl.BlockSpec((1, D_FF), const),
            resident((D_FF, D_MODEL)),
        ],
        out_specs=[
            pl.BlockSpec((tm, D_MODEL), row_map),
            pl.BlockSpec((1, FFN_CONV_K - 1, D_FF), mod_map),
        ],
        out_shape=[
            jax.ShapeDtypeStruct((n, D_MODEL), F32),
            jax.ShapeDtypeStruct((nb, FFN_CONV_K - 1, D_FF), F32),
        ],
        scratch_shapes=[
            pltpu.VMEM((8, D_FF), F32),
            pltpu.VMEM((tm, D_FF), BF16),
        ],
        compiler_params=pltpu.CompilerParams(dimension_semantics=("arbitrary",)),
        name="ffn_prompt",
    )(x, att, cvn, gate1, shift2, scale2, gate2, att_g, norm2_g, wo_bf, w1_bf, cw, cb, w2_bf)


def _ffn_sample_kernel(x_ref, att_ref, cvn_ref, g1_ref, sh_ref, sc_ref, g2_ref, ag_ref, ng_ref,
                       wo_ref, w1g_ref, w1u_ref, cw_ref, cb_ref, w2_ref, pm1_ref, pm2_ref,
                       o_ref, gout_ref, x1_ref, h2_ref, acc_ref, *, rows, seq):
    c = pl.program_id(0)

    @pl.when(c == 0)
    def _():
        x1 = _merge(x_ref[...], att_ref[...], cvn_ref[...], g1_ref[...], ag_ref[...], wo_ref)
        x1_ref[...] = x1
        h2_ref[...] = (_rms(x1, ng_ref[...]) * (1.0 + sc_ref[...]) + sh_ref[...]).astype(BF16)
        acc_ref[...] = jnp.zeros(acc_ref.shape, F32)

    h2 = h2_ref[...]
    g = _dot(h2, w1g_ref[...])
    up = _dot(h2, w1u_ref[...])
    t = lax.broadcasted_iota(jnp.int32, (rows, FF_CHUNK), 0) % seq
    gm1 = jnp.where(t == 0, pm1_ref[...], pltpu.roll(g, 1, 0))
    gm2 = jnp.where(t < 2, pm2_ref[...], pltpu.roll(g, 2, 0))
    y = cw_ref[0:1, :] * gm2 + cw_ref[1:2, :] * gm1 + cw_ref[2:3, :] * g + cb_ref[...]
    act = (_silu(y) * up).astype(BF16)
    acc_ref[...] += _dot(act, w2_ref[...])
    gout_ref[...] = g

    @pl.when(c == pl.num_programs(0) - 1)
    def _():
        o_ref[...] = x1_ref[...] + g2_ref[...] * acc_ref[...]


def _ffn_sample(x, att, cvn, gate1, shift2, scale2, gate2, att_g, norm2_g, wo_bf, w1_bf, cw, cb, w2_bf,
                pm1, pm2, *, seq):
    rows = x.shape[0]
    const = lambda c: (0, 0)
    full = lambda w: pl.BlockSpec((rows, w), const)
    chunk = lambda c: (0, c)
    return pl.pallas_call(
        functools.partial(_ffn_sample_kernel, rows=rows, seq=seq),
        grid=(N_FF_CHUNKS,),
        in_specs=[
            full(D_MODEL), full(ATT_WIDTH), full(CONV_WIDTH),
            full(D_MODEL), full(D_MODEL), full(D_MODEL), full(D_MODEL),
            pl.BlockSpec((1, ATT_WIDTH), const),
            pl.BlockSpec((1, D_MODEL), const),
            pl.BlockSpec((D_MODEL, D_MODEL), const),
            pl.BlockSpec((D_MODEL, FF_CHUNK), chunk),
            pl.BlockSpec((D_MODEL, FF_CHUNK), lambda c: (0, N_FF_CHUNKS + c)),
            pl.BlockSpec((FFN_CONV_K, FF_CHUNK), chunk),
            pl.BlockSpec((1, FF_CHUNK), chunk),
            pl.BlockSpec((FF_CHUNK, D_MODEL), lambda c: (c, 0)),
            pl.BlockSpec((rows, FF_CHUNK), chunk),
            pl.BlockSpec((rows, FF_CHUNK), chunk),
        ],
        out_specs=[
            pl.BlockSpec((rows, D_MODEL), const),
            pl.BlockSpec((rows, FF_CHUNK), chunk),
        ],
        out_shape=[
            jax.ShapeDtypeStruct((rows, D_MODEL), F32),
            jax.ShapeDtypeStruct((rows, D_FF), F32),
        ],
        scratch_shapes=[
            pltpu.VMEM((rows, D_MODEL), F32),
            pltpu.VMEM((rows, D_MODEL), BF16),
            pltpu.VMEM((rows, D_MODEL), F32),
        ],
        compiler_params=pltpu.CompilerParams(dimension_semantics=("arbitrary",)),
        name="ffn_sample",
    )(x, att, cvn, gate1, shift2, scale2, gate2, att_g, norm2_g, wo_bf, w1_bf, w1_bf, cw, cb, w2_bf, pm1, pm2)


def _page_mean_kernel(pt_ref, *refs):
    del pt_ref
    page_refs, o_ref = refs[:-1], refs[-1]
    ppb = MOBA_BLOCK // PAGE_SIZE
    for n in range(len(page_refs) // ppb):
        tot = page_refs[ppb * n][0, 0]
        for p in range(1, ppb):
            tot = tot + page_refs[ppb * n + p][0, 0]
        o_ref[0, 0, n] = jnp.sum(tot, axis=-1) * (1.0 / MOBA_BLOCK)


def _page_means(cache_kt, pt_flat, *, nb, n_pages):
    depth = cache_kt.shape[0]
    ppb = MOBA_BLOCK // PAGE_SIZE
    steps = n_pages // PAGES_PER_STEP
    bps = PAGES_PER_STEP // ppb

    def page_spec(p):
        return pl.BlockSpec((1, 1, ATT_HEADS, HEAD_DIM, PAGE_SIZE),
                            lambda l, b, s, pt: (l, pt[b * n_pages + s * PAGES_PER_STEP + p], 0, 0, 0))

    grid_spec = pltpu.PrefetchScalarGridSpec(
        num_scalar_prefetch=1,
        grid=(depth, nb, steps),
        in_specs=[page_spec(p) for p in range(PAGES_PER_STEP)],
        out_specs=pl.BlockSpec((1, 1, bps, ATT_HEADS, HEAD_DIM), lambda l, b, s, pt: (l, b, s, 0, 0)),
    )
    return pl.pallas_call(
        _page_mean_kernel,
        grid_spec=grid_spec,
        out_shape=jax.ShapeDtypeStruct((depth, nb, n_pages // ppb, ATT_HEADS, HEAD_DIM), F32),
        compiler_params=pltpu.CompilerParams(dimension_semantics=("arbitrary", "arbitrary", "arbitrary")),
        name="page_means",
    )(pt_flat, *([cache_kt] * PAGES_PER_STEP))


def _topk_sample_kernel(q_ref, km_ref, o_ref):
    rows, nblk = q_ref.shape[2], km_ref.shape[1]
    blane = lax.broadcasted_iota(jnp.int32, (rows, nblk), 1).astype(F32)
    olane = lax.broadcasted_iota(jnp.int32, (rows, LANES), 1)
    for hd in range(ATT_HEADS):
        qh = q_ref[hd, 0][:, 0:HEAD_DIM]
        gs = _dot3_nt(qh, km_ref[0, :, hd, :])
        _, firsts = _top3_select(gs, blane)
        out = jnp.zeros((rows, LANES), jnp.int32)
        for t, f in enumerate(firsts):
            out = jnp.where(olane == t, f.astype(jnp.int32), out)
        o_ref[0, hd] = out


def _topk_sample(q8, kmean):
    _, nb, rows, _ = q8.shape
    nblk = kmean.shape[1]
    return pl.pallas_call(
        _topk_sample_kernel,
        grid=(nb,),
        in_specs=[
            pl.BlockSpec((ATT_HEADS, 1, rows, LANES), lambda b: (0, b, 0, 0)),
            pl.BlockSpec((1, nblk, ATT_HEADS, HEAD_DIM), lambda b: (b, 0, 0, 0)),
        ],
        out_specs=pl.BlockSpec((1, ATT_HEADS, rows, LANES), lambda b: (b, 0, 0, 0)),
        out_shape=jax.ShapeDtypeStruct((nb, ATT_HEADS, rows, LANES), jnp.int32),
        compiler_params=pltpu.CompilerParams(dimension_semantics=("arbitrary",)),
        name="topk_sample",
    )(q8, kmean)


def _attn_sample_kernel(idx_ref, pt_ref, q_ref, kn_ref, vn_ref, *refs, dec_seq, n_sel_pages):
    del idx_ref, pt_ref
    kp_refs = refs[:n_sel_pages]
    vp_refs = refs[n_sel_pages:2 * n_sel_pages]
    o_ref = refs[2 * n_sel_pages]
    odd = pl.program_id(1) % 2
    q = (q_ref[0, 0][:, 0:HEAD_DIM] * (HEAD_DIM ** -0.5)).astype(BF16)
    rows = q.shape[0]
    lane = lax.broadcasted_iota(jnp.int32, (rows, LANES), 1)
    prow = lax.broadcasted_iota(jnp.int32, (rows, PAGE_SIZE), 0)
    pages_per_query = n_sel_pages // dec_seq

    ss = []
    for n in range(n_sel_pages):
        s = _dot(q, kp_refs[n][0, 0, 0].astype(BF16))
        ss.append(jnp.where(prow == n // pages_per_query, s, -jnp.inf))
    orow = lax.broadcasted_iota(jnp.int32, (rows, rows), 0)
    ocol = lax.broadcasted_iota(jnp.int32, (rows, rows), 1)
    s_own = _dot_nt(q, kn_ref[0, 0][:, 0:HEAD_DIM].astype(BF16))
    s_own = jnp.where((ocol <= orow) & (ocol < dec_seq), s_own, -jnp.inf)

    m = jnp.max(s_own, axis=1, keepdims=True)
    for s in ss:
        m = jnp.maximum(m, jnp.max(s, axis=1, keepdims=True))
    p_own = jnp.exp(s_own - m)
    den = jnp.sum(p_own, axis=1, keepdims=True)
    acc = _dot(p_own.astype(BF16), vn_ref[0, 0][:, 0:HEAD_DIM].astype(BF16))
    for n in range(n_sel_pages):
        p = jnp.exp(ss[n] - m)
        den = den + jnp.sum(p, axis=1, keepdims=True)
        acc = acc + _dot_nt(p.astype(BF16), vp_refs[n][0, 0, 0].astype(BF16))
    val = acc / den
    val2 = jnp.concatenate([val, val], axis=1)

    @pl.when(odd == 0)
    def _():
        o_ref[0] = val2

    @pl.when(odd == 1)
    def _():
        o_ref[0] = jnp.where(lane < HEAD_DIM, o_ref[0], val2)


def _attention_sample(idx_flat, pt_flat, q8, kn8, vn8, cache_kt, cache_vt, *, layer, nb, n_pages, dec_seq):
    ppb = MOBA_BLOCK // PAGE_SIZE
    rows = q8.shape[2]
    n_sel_pages = dec_seq * MOBA_TOPK * ppb

    def page_spec(n):
        qi, t, p = n // (MOBA_TOPK * ppb), (n // ppb) % MOBA_TOPK, n % ppb

        def imap(b, h, idx, pt):
            blk = idx[((b * ATT_HEADS + h) * dec_seq + qi) * MOBA_TOPK + t]
            return (layer, pt[b * n_pages + blk * ppb + p], h, 0, 0)

        return pl.BlockSpec((1, 1, 1, HEAD_DIM, PAGE_SIZE), imap)

    head = pl.BlockSpec((1, 1, rows, LANES), lambda b, h, idx, pt: (h, b, 0, 0))
    grid_spec = pltpu.PrefetchScalarGridSpec(
        num_scalar_prefetch=2,
        grid=(nb, ATT_HEADS),
        in_specs=[head, head, head] + [page_spec(n) for n in range(n_sel_pages)] * 2,
        out_specs=pl.BlockSpec((1, rows, LANES), lambda b, h, idx, pt: (b, 0, h // 2)),
    )
    return pl.pallas_call(
        functools.partial(_attn_sample_kernel, dec_seq=dec_seq, n_sel_pages=n_sel_pages),
        grid_spec=grid_spec,
        out_shape=jax.ShapeDtypeStruct((nb, rows, ATT_WIDTH), F32),
        compiler_params=pltpu.CompilerParams(dimension_semantics=("arbitrary", "arbitrary")),
        name="attention_sample",
    )(idx_flat, pt_flat, q8, kn8, vn8, *([cache_kt] * n_sel_pages), *([cache_vt] * n_sel_pages))


def _rope_tables(pos):
    half = ROT_DIM // 2
    inv = jnp.exp(-math.log(ROPE_THETA) * jnp.arange(half, dtype=F32) * (2.0 / ROT_DIM))
    ang = pos.astype(F32)[:, None] * inv[None, :]
    cos, sin = jnp.cos(ang), jnp.sin(ang)
    n = pos.shape[0]
    pad = jnp.zeros((n, HEAD_DIM - ROT_DIM), F32)
    zero = jnp.zeros((n, half), F32)
    cos_h = jnp.concatenate([cos, cos, pad + 1.0], axis=1)
    s1_h = jnp.concatenate([-sin, zero, pad], axis=1)
    s2_h = jnp.concatenate([zero, sin, pad], axis=1)
    two = lambda t: jnp.concatenate([t, t], axis=1)
    return two(cos_h), two(s1_h), two(s2_h)


def _pad_seq_rows(a, nbs, dec_seq, rows):
    a = a.reshape(a.shape[0], nbs, dec_seq, a.shape[-1])
    return jnp.pad(a, ((0, 0), (0, 0), (0, rows - dec_seq), (0, 0)))


def _sample_attend(q_hm, k_hm, v_hm, kmean_l, pt_flat, cache_kt, cache_vt, *, layer, nbs, dec_seq, n_pages):
    q8 = _pad_seq_rows(q_hm, nbs, dec_seq, 8)
    k8 = _pad_seq_rows(k_hm, nbs, dec_seq, 8)
    v8 = _pad_seq_rows(v_hm, nbs, dec_seq, 8)
    idx = _topk_sample(q8, kmean_l)
    idx_flat = idx[:, :, :dec_seq, :MOBA_TOPK].reshape(-1)
    o = _attention_sample(idx_flat, pt_flat, q8, k8, v8, cache_kt, cache_vt,
                          layer=layer, nb=nbs, n_pages=n_pages, dec_seq=dec_seq)
    return o[:, :dec_seq].reshape(nbs * dec_seq, ATT_WIDTH)


def kernel(x_prompt, x_sample, cache_k, cache_v, state_conv, state_ffn, page_table, c_prompt, c_sample,
           norm1_g, norm2_g, w_ada, b_ada, w_in, q_norm_g, k_norm_g, conv_w, conv_b, conv_ln_g, conv_ln_b,
           attn_out_g, conv_out_g, w_out, w_fc1, ffn_conv_w, ffn_conv_b, w_fc2):
    nb, seq, _ = x_prompt.shape
    nbs, dec_seq, _ = x_sample.shape
    depth = w_in.shape[0]
    n_pages = page_table.shape[1]
    past = n_pages * PAGE_SIZE
    n_pool = cache_k.shape[1]
    n_rows_s = nbs * dec_seq
    assert past % MOBA_BLOCK == 0 and past // MOBA_BLOCK >= MOBA_TOPK
    assert seq % TM_IN == 0 and seq % TM_FFN == 0 and TM_IN % MOBA_BLOCK == 0 and n_pages % PAGES_PER_STEP == 0

    n_c = nb + nbs
    c_rows = -(-n_c // 8) * 8
    c_all = jnp.pad(jnp.concatenate([c_prompt, c_sample], axis=0), ((0, c_rows - n_c), (0, 0)))
    m_all = _modulation(c_all, w_ada, b_ada)

    w_in_bf, w_out_bf = w_in.astype(BF16), w_out.astype(BF16)
    w_fc1_bf, w_fc2_bf = w_fc1.astype(BF16), w_fc2.astype(BF16)
    head_id = jnp.arange(ATT_WIDTH) // HEAD_DIM
    bd = jnp.where(head_id[:, None] == head_id[None, :], 1.0 / HEAD_DIM, 0.0).astype(BF16)

    tabs_p = _rope_tables(jnp.arange(seq))
    tabs_s = _rope_tables(past + (jnp.arange(n_rows_s) % dec_seq))

    cache_kt = cache_k.transpose(0, 1, 3, 4, 2)
    cache_vt = cache_v.transpose(0, 1, 3, 4, 2)
    pt_flat = page_table.reshape(-1)
    kmean_s = _page_means(cache_kt, pt_flat, nb=nbs, n_pages=n_pages)

    xp = x_prompt.reshape(nb * seq, D_MODEL)
    xs = x_sample.reshape(n_rows_s, D_MODEL)
    outs = [[] for _ in range(8)]
    for l in range(depth):
        row = lambda a: a[l].reshape(1, -1)
        mp = m_all[l, :nb].reshape(nb, N_MOD, 1, D_MODEL)
        ms = jnp.repeat(m_all[l, nb:n_c].reshape(nbs, N_MOD, D_MODEL), dec_seq, axis=0)
        mp = [mp[:, i] for i in range(N_MOD)]
        ms = [ms[:, i] for i in range(N_MOD)]
        qg = jnp.tile(q_norm_g[l], ATT_HEADS).reshape(1, -1)
        kg = jnp.tile(k_norm_g[l], ATT_HEADS).reshape(1, -1)

        q, k, v, u, kaug, vaug, kmean = _mixer_in(
            xp, mp[0], mp[1], row(norm1_g), w_in_bf[l], bd, qg, kg, *tabs_p, prompt=True, tm=TM_IN, seq=seq)
        nblk = seq // MOBA_BLOCK
        km = kmean.reshape(nb, nblk, ATT_HEADS, HEAD_DIM).transpose(0, 2, 1, 3)
        kmw = jnp.pad(km, ((0, 0), (0, 0), (0, 0), (0, LANES - HEAD_DIM)))
        att = _attention_prompt(q, kaug, vaug, kmw, nb=nb, seq=seq)
        cvn = _conv_prompt(u, conv_w[l], row(conv_b), row(conv_ln_g), row(conv_ln_b), row(conv_out_g),
                           nb=nb, seq=seq)
        xp_new, ffn_state_p = _ffn_prompt(
            xp, att.reshape(nb * seq, ATT_WIDTH), cvn.reshape(nb * seq, CONV_WIDTH),
            mp[2], mp[3], mp[4], mp[5], row(attn_out_g), row(norm2_g),
            w_out_bf[l], w_fc1_bf[l], ffn_conv_w[l], row(ffn_conv_b), w_fc2_bf[l], nb=nb, seq=seq)
        outs[0].append(k.reshape(nb, seq, ATT_HEADS, HEAD_DIM))
        outs[1].append(v.reshape(nb, seq, ATT_HEADS, HEAD_DIM))
        outs[2].append(u.reshape(nb, seq, CONV_WIDTH)[:, seq - (CONV_K - 1):])
        outs[3].append(ffn_state_p)
        xp = xp_new

        _, ks, vs, us, q_hm, k_hm, v_hm = _mixer_in(
            xs, ms[0][None], ms[1][None], row(norm1_g), w_in_bf[l], bd, qg, kg, *tabs_s,
            prompt=False, tm=n_rows_s, seq=n_rows_s)
        att_s = _sample_attend(q_hm, k_hm, v_hm, kmean_s[l], pt_flat, cache_kt, cache_vt,
                               layer=l, nbs=nbs, dec_seq=dec_seq, n_pages=n_pages)
        u_ext = jnp.concatenate([state_conv[l], us.reshape(nbs, dec_seq, CONV_WIDTH)], axis=1)
        cvn_s = _conv_sample(u_ext, conv_w[l], row(conv_b), row(conv_ln_g), row(conv_ln_b), row(conv_out_g))
        st = state_ffn[l]
        pm1 = jnp.repeat(st[:, 1], dec_seq, axis=0)
        pm2 = jnp.stack([st[:, 0], st[:, 1]] + [st[:, 1]] * (dec_seq - 2), axis=1).reshape(n_rows_s, D_FF)
        xs_new, g_s = _ffn_sample(
            xs, att_s, cvn_s.reshape(n_rows_s, CONV_WIDTH), ms[2], ms[3], ms[4], ms[5],
            row(attn_out_g), row(norm2_g), w_out_bf[l], w_fc1_bf[l], ffn_conv_w[l], row(ffn_conv_b),
            w_fc2_bf[l], pm1, pm2, seq=dec_seq)
        outs[4].append(ks.reshape(nbs, dec_seq, ATT_HEADS, HEAD_DIM))
        outs[5].append(vs.reshape(nbs, dec_seq, ATT_HEADS, HEAD_DIM))
        outs[6].append(u_ext[:, dec_seq:])
        outs[7].append(g_s.reshape(nbs, dec_seq, D_FF)[:, dec_seq - (FFN_CONV_K - 1):])
        xs = xs_new

    stk = [jnp.stack(o) for o in outs]
    return (xp.reshape(nb, seq, D_MODEL), xs.reshape(nbs, dec_seq, D_MODEL),
            stk[0], stk[1], stk[2], stk[3], stk[4], stk[5], stk[6], stk[7])
```

```python
import functools
import math

import jax
import jax.numpy as jnp
from jax import lax
from jax.experimental import pallas as pl
from jax.experimental.pallas import tpu as pltpu

F32 = jnp.float32
BF16 = jnp.bfloat16

D_MODEL = 1024
HEAD_DIM = 64
ATT_HEADS = 8
ATT_WIDTH = 512
CONV_WIDTH = 512
CONV_K = 31
ROT_DIM = 16
ROPE_THETA = 500000.0
MOBA_BLOCK = 256
MOBA_TOPK = 3
PAGE_SIZE = 128
D_FF = 2816
FFN_CONV_K = 3
N_MOD = 6
EPS = 1e-6
IN_WIDTH = 3 * ATT_WIDTH + 2 * CONV_WIDTH

LANES = 128
NEG_BIG = -1e30
FF_CHUNK = 256
N_FF_CHUNKS = D_FF // FF_CHUNK
PAGES_PER_STEP = 32

TM_IN = 512
TM_FFN = 512
TC_CONV = 256
CONV_HALO = 32
CONV_ROWS = 64
ATT_KEY_BLOCKS = 4


def _dot(a, b):
    return jnp.dot(a, b, preferred_element_type=F32)


def _dot_nt(a, b):
    return lax.dot_general(a, b, (((1,), (1,)), ((), ())), preferred_element_type=F32)


def _split(a):
    hi = a.astype(BF16)
    lo = (a - hi.astype(F32)).astype(BF16)
    return hi, lo


def _dot3(a, b):
    ah, al = _split(a)
    bh, bl = _split(b)
    return _dot(ah, bh) + _dot(al, bh) + _dot(ah, bl)


def _dot3_nt(a, b):
    ah, al = _split(a)
    bh, bl = _split(b)
    return _dot_nt(ah, bh) + _dot_nt(al, bh) + _dot_nt(ah, bl)


def _sigmoid(x):
    return 1.0 / (1.0 + jnp.exp(-x))


def _silu(x):
    return x * _sigmoid(x)


def _rms(x, g):
    ms = jnp.mean(x * x, axis=-1, keepdims=True)
    return x * lax.rsqrt(ms + EPS) * g


def _mod_kernel(c_ref, w_ref, b_ref, o_ref):
    a = _silu(c_ref[...])
    o_ref[0] = _dot3(a, w_ref[0]) + b_ref[0]


def _modulation(c_all, w_ada, b_ada):
    depth = w_ada.shape[0]
    rows = c_all.shape[0]
    nt = (N_MOD * D_MODEL) // D_MODEL
    return pl.pallas_call(
        _mod_kernel,
        grid=(depth, nt),
        in_specs=[
            pl.BlockSpec((rows, D_MODEL), lambda l, n: (0, 0)),
            pl.BlockSpec((1, D_MODEL, D_MODEL), lambda l, n: (l, 0, n)),
            pl.BlockSpec((1, 1, D_MODEL), lambda l, n: (l, 0, n)),
        ],
        out_specs=pl.BlockSpec((1, rows, D_MODEL), lambda l, n: (l, 0, n)),
        out_shape=jax.ShapeDtypeStruct((depth, rows, N_MOD * D_MODEL), F32),
        name="modulation",
    )(c_all, w_ada, b_ada.reshape(depth, 1, N_MOD * D_MODEL))


def _tile_lanes(t, n):
    return jnp.concatenate([t] * n, axis=1)


def _mixer_in_kernel(x_ref, sh_ref, sc_ref, ng_ref, w_ref, bd_ref, qg_ref, kg_ref,
                     cos_ref, s1_ref, s2_ref, k_ref, v_ref, u_ref, *extra,
                     prompt, tm, tiles_per_seq):
    x = x_ref[...]
    h = _rms(x, ng_ref[...])
    h = h * (1.0 + sc_ref[0]) + sh_ref[0]
    z = _dot(h.astype(BF16), w_ref[...])

    reps = ATT_WIDTH // LANES
    cos = _tile_lanes(cos_ref[...], reps)
    s1 = _tile_lanes(s1_ref[...], reps)
    s2 = _tile_lanes(s2_ref[...], reps)
    bd = bd_ref[...]

    def head_norm_rope(t, g):
        hi, lo = _split(t * t)
        ms = _dot(hi, bd) + _dot(lo, bd)
        tn = t * lax.rsqrt(ms + EPS) * g
        return (tn * cos + pltpu.roll(tn, ATT_WIDTH - ROT_DIM // 2, 1) * s1
                + pltpu.roll(tn, ROT_DIM // 2, 1) * s2)

    q = head_norm_rope(z[:, 0:ATT_WIDTH], qg_ref[...])
    k = head_norm_rope(z[:, ATT_WIDTH:2 * ATT_WIDTH], kg_ref[...])
    v = z[:, 2 * ATT_WIDTH:3 * ATT_WIDTH]
    a = z[:, 3 * ATT_WIDTH:3 * ATT_WIDTH + CONV_WIDTH]
    g = z[:, 3 * ATT_WIDTH + CONV_WIDTH:]
    k_ref[...] = k
    v_ref[...] = v
    u_ref[...] = a * _sigmoid(g)

    if prompt:
        ka_ref, va_ref, qa_ref, km_ref = extra
        nblk = km_ref.shape[0]
        t_in_seq = pl.program_id(0) % tiles_per_seq
        lane = lax.broadcasted_iota(jnp.int32, (tm, LANES), 1)
        row = lax.broadcasted_iota(jnp.int32, (tm, LANES), 0)
        blk = (t_in_seq * tm + row) // MOBA_BLOCK
        onehot = jnp.where(lane - HEAD_DIM == blk, 1.0, 0.0)

        @pl.when(t_in_seq == 0)
        def _():
            km_ref[...] = jnp.zeros(km_ref.shape, F32)

        bpt = tm // MOBA_BLOCK
        for c in range(bpt):
            km_ref[pl.ds(t_in_seq * bpt + c, 1), :] = (
                jnp.sum(k[c * MOBA_BLOCK:(c + 1) * MOBA_BLOCK], axis=0, keepdims=True) * (1.0 / MOBA_BLOCK))
        km = km_ref[...]

        kblk = lax.broadcasted_iota(jnp.int32, (nblk, tm), 0)
        kblkf = kblk.astype(F32)
        qblk = (t_in_seq * tm + lax.broadcasted_iota(jnp.int32, (nblk, tm), 1)) // MOBA_BLOCK
        past = kblk < qblk
        klane = lax.broadcasted_iota(jnp.int32, (nblk, LANES), 1)
        place = jnp.where(klane == lax.broadcasted_iota(jnp.int32, (nblk, LANES), 0) + HEAD_DIM,
                          1.0, 0.0).astype(BF16)
        for hd in range(ATT_HEADS):
            lo_l = LANES * (hd // 2)
            qt = q[:, lo_l:lo_l + LANES]
            kt = k[:, lo_l:lo_l + LANES]
            vt = v[:, lo_l:lo_l + LANES]
            kmt = km[:, lo_l:lo_l + LANES]
            if hd % 2 == 1:
                qt = pltpu.roll(qt, HEAD_DIM, 1)
                kt = pltpu.roll(kt, HEAD_DIM, 1)
                vt = pltpu.roll(vt, HEAD_DIM, 1)
                kmt = pltpu.roll(kmt, HEAD_DIM, 1)
            ka_ref[0, hd] = jnp.where(lane < HEAD_DIM, kt, onehot).astype(BF16)
            va_ref[0, hd] = jnp.where(lane < HEAD_DIM, vt, 1.0).astype(BF16)
            gs = _dot3_nt(jnp.where(klane < HEAD_DIM, kmt, 0.0), qt)
            selm, _ = _top3_select(jnp.where(past, gs, -jnp.inf), kblkf, axis=0)
            bias_t = jnp.where(past & (selm == 0.0), NEG_BIG, 0.0).astype(BF16)
            bias = lax.dot_general(bias_t, place, (((0,), (0,)), ((), ())), preferred_element_type=F32)
            qa_ref[0, hd] = jnp.where(lane < HEAD_DIM, qt * (HEAD_DIM ** -0.5), bias).astype(BF16)
    else:
        lane = lax.broadcasted_iota(jnp.int32, (tm, LANES), 1)
        for src, dst in zip((q, k, v), extra):
            for hd in range(ATT_HEADS):
                lo_l = LANES * (hd // 2)
                t = src[:, lo_l:lo_l + LANES]
                if hd % 2 == 1:
                    t = pltpu.roll(t, HEAD_DIM, 1)
                dst[hd] = jnp.where(lane < HEAD_DIM, t, 0.0)


def _mixer_in(x, shift, scale, norm_g, w_in_bf, bd, qg, kg, cos_t, s1_t, s2_t, *, prompt, tm, seq):
    n = x.shape[0]
    nt = n // tm
    tiles_per_seq = seq // tm
    mod_rows = shift.shape[1]
    if prompt:
        mod_map = lambda t: (t // tiles_per_seq, 0, 0)
        tab_map = lambda t: (t % tiles_per_seq, 0)
    else:
        mod_map = lambda t: (t, 0, 0)
        tab_map = lambda t: (t, 0)
    const = lambda t: (0, 0)
    row_map = lambda t: (t, 0)
    in_specs = [
        pl.BlockSpec((tm, D_MODEL), row_map),
        pl.BlockSpec((1, mod_rows, D_MODEL), mod_map),
        pl.BlockSpec((1, mod_rows, D_MODEL), mod_map),
        pl.BlockSpec((1, D_MODEL), const),
        pl.BlockSpec((D_MODEL, IN_WIDTH), const),
        pl.BlockSpec((ATT_WIDTH, ATT_WIDTH), const),
        pl.BlockSpec((1, ATT_WIDTH), const),
        pl.BlockSpec((1, ATT_WIDTH), const),
        pl.BlockSpec((tm, LANES), tab_map),
        pl.BlockSpec((tm, LANES), tab_map),
        pl.BlockSpec((tm, LANES), tab_map),
    ]
    out_specs = [pl.BlockSpec((tm, ATT_WIDTH), row_map)] * 3
    out_shape = [jax.ShapeDtypeStruct((n, ATT_WIDTH), F32)] * 3
    scratch = []
    if prompt:
        nb = n // seq
        aug_map = lambda t: (t // tiles_per_seq, 0, t % tiles_per_seq, 0)
        out_specs += [pl.BlockSpec((1, ATT_HEADS, tm, LANES), aug_map)] * 3
        out_shape += [jax.ShapeDtypeStruct((nb, ATT_HEADS, seq, LANES), BF16)] * 3
        scratch = [pltpu.VMEM((seq // MOBA_BLOCK, ATT_WIDTH), F32)]
    else:
        out_specs += [pl.BlockSpec((ATT_HEADS, tm, LANES), lambda t: (0, t, 0))] * 3
        out_shape += [jax.ShapeDtypeStruct((ATT_HEADS, n, LANES), F32)] * 3
    return pl.pallas_call(
        functools.partial(_mixer_in_kernel, prompt=prompt, tm=tm, tiles_per_seq=tiles_per_seq),
        grid=(nt,),
        in_specs=in_specs,
        out_specs=out_specs,
        out_shape=out_shape,
        scratch_shapes=scratch,
        compiler_params=pltpu.CompilerParams(dimension_semantics=("arbitrary",)),
        name="mixer_in_prompt" if prompt else "mixer_in_sample",
    )(x, shift, scale, norm_g, w_in_bf, bd, qg, kg, cos_t, s1_t, s2_t)


def _top3_select(cur, lanef, axis=1):
    selm = jnp.zeros(cur.shape, F32)
    firsts = []
    for _ in range(MOBA_TOPK):
        m = jnp.max(cur, axis=axis, keepdims=True)
        first = jnp.min(jnp.where(cur == m, lanef, 1e9), axis=axis, keepdims=True)
        pick = (lanef == first) & (m > -jnp.inf)
        selm = jnp.where(pick, 1.0, selm)
        cur = jnp.where(pick, -jnp.inf, cur)
        firsts.append(first)
    return selm, firsts


def _attn_kernel(qa_ref, ka_ref, va_ref, o_ref, *, kb):
    i = pl.program_id(2)
    tq = MOBA_BLOCK
    lane = lax.broadcasted_iota(jnp.int32, (tq, LANES), 1)
    gw = kb * tq
    g_own = i // kb
    row = lax.broadcasted_iota(jnp.int32, (tq, gw), 0)
    col = lax.broadcasted_iota(jnp.int32, (tq, gw), 1)
    causal = col - row <= (i - g_own * kb) * tq
    qas = [qa_ref[0, 0], qa_ref[0, 1]]

    def group_rows(g):
        return pl.ds(pl.multiple_of(g * gw, gw), gw)

    init = []
    for hh in range(2):
        s = _dot_nt(qas[hh], ka_ref[0, hh, group_rows(g_own), :])
        s = jnp.where(causal, s, -jnp.inf)
        m = jnp.max(s, axis=1, keepdims=True)
        p = jnp.exp(s - m)
        acc = _dot(p.astype(BF16), va_ref[0, hh, group_rows(g_own), :])
        init.append((m, acc))

    def body(g, carry):
        new = []
        for hh in range(2):
            m, acc = carry[hh]
            s = _dot_nt(qas[hh], ka_ref[0, hh, group_rows(g), :])
            mn = jnp.maximum(m, jnp.max(s, axis=1, keepdims=True))
            p = jnp.exp(s - mn)
            acc = acc * jnp.exp(m - mn) + _dot(p.astype(BF16), va_ref[0, hh, group_rows(g), :])
            new.append((mn, acc))
        return tuple(new)

    (_, acc0), (_, acc1) = lax.fori_loop(0, g_own, body, tuple(init))
    o0 = acc0 / pltpu.roll(acc0, HEAD_DIM, 1)
    o1 = pltpu.roll(acc1, HEAD_DIM, 1) / acc1
    o_ref[0] = jnp.where(lane < HEAD_DIM, o0, o1)


def _attention_prompt(qaug, kaug, vaug, *, nb, seq):
    nq = seq // MOBA_BLOCK
    hp = ATT_HEADS // 2
    kb = math.gcd(ATT_KEY_BLOCKS, nq)
    return pl.pallas_call(
        functools.partial(_attn_kernel, kb=kb),
        grid=(nb, hp, nq),
        in_specs=[
            pl.BlockSpec((1, 2, MOBA_BLOCK, LANES), lambda b, h, i: (b, h, i, 0)),
            pl.BlockSpec((1, 2, seq, LANES), lambda b, h, i: (b, h, 0, 0)),
            pl.BlockSpec((1, 2, seq, LANES), lambda b, h, i: (b, h, 0, 0)),
        ],
        out_specs=pl.BlockSpec((1, MOBA_BLOCK, LANES), lambda b, h, i: (b, i, h)),
        out_shape=jax.ShapeDtypeStruct((nb, seq, ATT_WIDTH), F32),
        compiler_params=pltpu.CompilerParams(dimension_semantics=("arbitrary", "arbitrary", "arbitrary")),
        name="attention_prompt",
    )(qaug, kaug, vaug)


def _conv_post(y, lg, lb, cg):
    mu = jnp.mean(y, axis=-1, keepdims=True)
    yc = y - mu
    var = jnp.mean(yc * yc, axis=-1, keepdims=True)
    cv = _silu(yc * lax.rsqrt(var + EPS) * lg + lb)
    return _rms(cv, cg)


def _conv_kernel(prev_ref, cur_ref, w_ref, b_ref, lg_ref, lb_ref, cg_ref, o_ref, ext_ref, *, tc):
    t = pl.program_id(1)
    ext_ref[0:CONV_HALO, :] = jnp.where(t > 0, prev_ref[0], 0.0)
    ext_ref[CONV_HALO:CONV_HALO + tc, :] = cur_ref[0]
    off = CONV_HALO - (CONV_K - 1)
    for c in range(tc // CONV_ROWS):
        acc = jnp.zeros((CONV_ROWS, CONV_WIDTH), F32)
        for j in range(CONV_K):
            acc = acc + w_ref[j:j + 1, :] * ext_ref[pl.ds(c * CONV_ROWS + off + j, CONV_ROWS), :]
        y = acc + b_ref[...]
        o_ref[0, c * CONV_ROWS:(c + 1) * CONV_ROWS, :] = _conv_post(
            y, lg_ref[...], lb_ref[...], cg_ref[...]).astype(o_ref.dtype)


def _conv_prompt(u, conv_w, conv_b, ln_g, ln_b, cg, *, nb, seq):
    tc = TC_CONV
    nt = seq // tc
    hpt = tc // CONV_HALO
    vec = pl.BlockSpec((1, CONV_WIDTH), lambda b, t: (0, 0))
    u3 = u.reshape(nb, seq, CONV_WIDTH)
    return pl.pallas_call(
        functools.partial(_conv_kernel, tc=tc),
        grid=(nb, nt),
        in_specs=[
            pl.BlockSpec((1, CONV_HALO, CONV_WIDTH), lambda b, t: (b, jnp.maximum(t * hpt - 1, 0), 0)),
            pl.BlockSpec((1, tc, CONV_WIDTH), lambda b, t: (b, t, 0)),
            pl.BlockSpec((CONV_K, CONV_WIDTH), lambda b, t: (0, 0)),
            vec, vec, vec, vec,
        ],
        out_specs=pl.BlockSpec((1, tc, CONV_WIDTH), lambda b, t: (b, t, 0)),
        out_shape=jax.ShapeDtypeStruct((nb, seq, CONV_WIDTH), BF16),
        scratch_shapes=[pltpu.VMEM((CONV_HALO + tc, CONV_WIDTH), F32)],
        compiler_params=pltpu.CompilerParams(dimension_semantics=("arbitrary", "arbitrary")),
        name="conv_prompt",
    )(u3, u3, conv_w, conv_b, ln_g, ln_b, cg)


def _conv_sample_kernel(ue_ref, w_ref, b_ref, lg_ref, lb_ref, cg_ref, o_ref, *, rows):
    ue = ue_ref[0]
    w = w_ref[...]
    ys = [jnp.sum(ue[t:t + CONV_K, :] * w, axis=0, keepdims=True) for t in range(rows)]
    y = jnp.concatenate(ys, axis=0) + b_ref[...]
    o_ref[0] = _conv_post(y, lg_ref[...], lb_ref[...], cg_ref[...]).astype(o_ref.dtype)


def _conv_sample(u_ext, conv_w, conv_b, ln_g, ln_b, cg):
    nb, ext_rows, _ = u_ext.shape
    rows = ext_rows - (CONV_K - 1)
    vec = pl.BlockSpec((1, CONV_WIDTH), lambda b: (0, 0))
    return pl.pallas_call(
        functools.partial(_conv_sample_kernel, rows=rows),
        grid=(nb,),
        in_specs=[
            pl.BlockSpec((1, ext_rows, CONV_WIDTH), lambda b: (b, 0, 0)),
            pl.BlockSpec((CONV_K, CONV_WIDTH), lambda b: (0, 0)),
            vec, vec, vec, vec,
        ],
        out_specs=pl.BlockSpec((1, rows, CONV_WIDTH), lambda b: (b, 0, 0)),
        out_shape=jax.ShapeDtypeStruct((nb, rows, CONV_WIDTH), F32),
        compiler_params=pltpu.CompilerParams(dimension_semantics=("arbitrary",)),
        name="conv_sample",
    )(u_ext, conv_w, conv_b, ln_g, ln_b, cg)


def _merge(x, att, cvn, g1, ag, wo_ref):
    attn = _rms(att, ag).astype(BF16)
    merged = _dot(attn, wo_ref[0:ATT_WIDTH, :]) + _dot(cvn.astype(BF16), wo_ref[ATT_WIDTH:, :])
    return x + g1 * merged


def _ffn_prompt_kernel(x_ref, att_ref, cvn_ref, g1_ref, sh_ref, sc_ref, g2_ref, ag_ref, ng_ref,
                       wo_ref, w1_ref, cw_ref, cb_ref, w2_ref, o_ref, st_ref, carry_ref, act_ref,
                       *, tm, tiles_per_seq):
    t_in_seq = pl.program_id(0) % tiles_per_seq

    @pl.when(t_in_seq == 0)
    def _():
        carry_ref[...] = jnp.zeros(carry_ref.shape, F32)

    x1 = _merge(x_ref[...], att_ref[...], cvn_ref[...], g1_ref[0], ag_ref[...], wo_ref)
    h2 = (_rms(x1, ng_ref[...]) * (1.0 + sc_ref[0]) + sh_ref[0]).astype(BF16)
    row = lax.broadcasted_iota(jnp.int32, (tm, FF_CHUNK), 0)
    for c in range(N_FF_CHUNKS):
        cols = slice(c * FF_CHUNK, (c + 1) * FF_CHUNK)
        g = _dot(h2, w1_ref[:, cols])
        up = _dot(h2, w1_ref[:, D_FF + c * FF_CHUNK:D_FF + (c + 1) * FF_CHUNK])
        p0 = carry_ref[0:1, cols]
        p1 = carry_ref[1:2, cols]
        gm1 = jnp.where(row == 0, p1, pltpu.roll(g, 1, 0))
        gm2 = jnp.where(row == 0, p0, jnp.where(row == 1, p1, pltpu.roll(g, 2, 0)))
        y = cw_ref[0:1, cols] * gm2 + cw_ref[1:2, cols] * gm1 + cw_ref[2:3, cols] * g + cb_ref[:, cols]
        act_ref[:, cols] = (_silu(y) * up).astype(BF16)
        tail = g[tm - (FFN_CONV_K - 1):tm, :]
        carry_ref[0:FFN_CONV_K - 1, cols] = tail
        st_ref[0, :, cols] = tail
    o_ref[...] = x1 + g2_ref[0] * _dot(act_ref[...], w2_ref[...])


def _ffn_prompt(x, att, cvn, gate1, shift2, scale2, gate2, att_g, norm2_g, wo_bf, w1_bf, cw, cb, w2_bf,
                *, nb, seq):
    tm = TM_FFN
    n = x.shape[0]
    nt = n // tm
    tiles_per_seq = seq // tm
    row_map = lambda t: (t, 0)
    mod_map = lambda t: (t // tiles_per_seq, 0, 0)
    const = lambda t: (0, 0)
    mod = pl.BlockSpec((1, 1, D_MODEL), mod_map)

    def resident(shape):
        return pl.BlockSpec(shape, const, pipeline_mode=pl.Buffered(1))

    return pl.pallas_call(
        functools.partial(_ffn_prompt_kernel, tm=tm, tiles_per_seq=tiles_per_seq),
        grid=(nt,),
        in_specs=[
            pl.BlockSpec((tm, D_MODEL), row_map),
            pl.BlockSpec((tm, ATT_WIDTH), row_map),
            pl.BlockSpec((tm, CONV_WIDTH), row_map),
            mod, mod, mod, mod,
            pl.BlockSpec((1, ATT_WIDTH), const),
            pl.BlockSpec((1, D_MODEL), const),
            resident((D_MODEL, D_MODEL)),
            resident((D_MODEL, 2 * D_FF)),
            pl.BlockSpec((FFN_CONV_K, D_FF), const),
            pl.BlockSpec((1, D_FF), const),
            resident((D_FF, D_MODEL)),
        ],
        out_specs=[
            pl.BlockSpec((tm, D_MODEL), row_map),
            pl.BlockSpec((1, FFN_CONV_K - 1, D_FF), mod_map),
        ],
        out_shape=[
            jax.ShapeDtypeStruct((n, D_MODEL), F32),
            jax.ShapeDtypeStruct((nb, FFN_CONV_K - 1, D_FF), F32),
        ],
        scratch_shapes=[
            pltpu.VMEM((8, D_FF), F32),
            pltpu.VMEM((tm, D_FF), BF16),
        ],
        compiler_params=pltpu.CompilerParams(dimension_semantics=("arbitrary",)),
        name="ffn_prompt",
    )(x, att, cvn, gate1, shift2, scale2, gate2, att_g, norm2_g, wo_bf, w1_bf, cw, cb, w2_bf)


def _ffn_sample_kernel(x_ref, att_ref, cvn_ref, g1_ref, sh_ref, sc_ref, g2_ref, ag_ref, ng_ref,
                       wo_ref, w1g_ref, w1u_ref, cw_ref, cb_ref, w2_ref, pm1_ref, pm2_ref,
                       o_ref, gout_ref, x1_ref, h2_ref, acc_ref, *, rows, seq):
    c = pl.program_id(0)

    @pl.when(c == 0)
    def _():
        x1 = _merge(x_ref[...], att_ref[...], cvn_ref[...], g1_ref[...], ag_ref[...], wo_ref)
        x1_ref[...] = x1
        h2_ref[...] = (_rms(x1, ng_ref[...]) * (1.0 + sc_ref[...]) + sh_ref[...]).astype(BF16)
        acc_ref[...] = jnp.zeros(acc_ref.shape, F32)

    h2 = h2_ref[...]
    g = _dot(h2, w1g_ref[...])
    up = _dot(h2, w1u_ref[...])
    t = lax.broadcasted_iota(jnp.int32, (rows, FF_CHUNK), 0) % seq
    gm1 = jnp.where(t == 0, pm1_ref[...], pltpu.roll(g, 1, 0))
    gm2 = jnp.where(t < 2, pm2_ref[...], pltpu.roll(g, 2, 0))
    y = cw_ref[0:1, :] * gm2 + cw_ref[1:2, :] * gm1 + cw_ref[2:3, :] * g + cb_ref[...]
    act = (_silu(y) * up).astype(BF16)
    acc_ref[...] += _dot(act, w2_ref[...])
    gout_ref[...] = g

    @pl.when(c == pl.num_programs(0) - 1)
    def _():
        o_ref[...] = x1_ref[...] + g2_ref[...] * acc_ref[...]


def _ffn_sample(x, att, cvn, gate1, shift2, scale2, gate2, att_g, norm2_g, wo_bf, w1_bf, cw, cb, w2_bf,
                pm1, pm2, *, seq):
    rows = x.shape[0]
    const = lambda c: (0, 0)
    full = lambda w: pl.BlockSpec((rows, w), const)
    chunk = lambda c: (0, c)
    return pl.pallas_call(
        functools.partial(_ffn_sample_kernel, rows=rows, seq=seq),
        grid=(N_FF_CHUNKS,),
        in_specs=[
            full(D_MODEL), full(ATT_WIDTH), full(CONV_WIDTH),
            full(D_MODEL), full(D_MODEL), full(D_MODEL), full(D_MODEL),
            pl.BlockSpec((1, ATT_WIDTH), const),
            pl.BlockSpec((1, D_MODEL), const),
            pl.BlockSpec((D_MODEL, D_MODEL), const),
            pl.BlockSpec((D_MODEL, FF_CHUNK), chunk),
            pl.BlockSpec((D_MODEL, FF_CHUNK), lambda c: (0, N_FF_CHUNKS + c)),
            pl.BlockSpec((FFN_CONV_K, FF_CHUNK), chunk),
            pl.BlockSpec((1, FF_CHUNK), chunk),
            pl.BlockSpec((FF_CHUNK, D_MODEL), lambda c: (c, 0)),
            pl.BlockSpec((rows, FF_CHUNK), chunk),
            pl.BlockSpec((rows, FF_CHUNK), chunk),
        ],
        out_specs=[
            pl.BlockSpec((rows, D_MODEL), const),
            pl.BlockSpec((rows, FF_CHUNK), chunk),
        ],
        out_shape=[
            jax.ShapeDtypeStruct((rows, D_MODEL), F32),
            jax.ShapeDtypeStruct((rows, D_FF), F32),
        ],
        scratch_shapes=[
            pltpu.VMEM((rows, D_MODEL), F32),
            pltpu.VMEM((rows, D_MODEL), BF16),
            pltpu.VMEM((rows, D_MODEL), F32),
        ],
        compiler_params=pltpu.CompilerParams(dimension_semantics=("arbitrary",)),
        name="ffn_sample",
    )(x, att, cvn, gate1, shift2, scale2, gate2, att_g, norm2_g, wo_bf, w1_bf, w1_bf, cw, cb, w2_bf, pm1, pm2)


def _page_mean_kernel(pt_ref, *refs):
    del pt_ref
    page_refs, o_ref = refs[:-1], refs[-1]
    ppb = MOBA_BLOCK // PAGE_SIZE
    for n in range(len(page_refs) // ppb):
        tot = page_refs[ppb * n][0, 0]
        for p in range(1, ppb):
            tot = tot + page_refs[ppb * n + p][0, 0]
        o_ref[0, 0, n] = jnp.sum(tot, axis=-1) * (1.0 / MOBA_BLOCK)


def _page_means(cache_kt, pt_flat, *, nb, n_pages):
    depth = cache_kt.shape[0]
    ppb = MOBA_BLOCK // PAGE_SIZE
    steps = n_pages // PAGES_PER_STEP
    bps = PAGES_PER_STEP // ppb

    def page_spec(p):
        return pl.BlockSpec((1, 1, ATT_HEADS, HEAD_DIM, PAGE_SIZE),
                            lambda l, b, s, pt: (l, pt[b * n_pages + s * PAGES_PER_STEP + p], 0, 0, 0))

    grid_spec = pltpu.PrefetchScalarGridSpec(
        num_scalar_prefetch=1,
        grid=(depth, nb, steps),
        in_specs=[page_spec(p) for p in range(PAGES_PER_STEP)],
        out_specs=pl.BlockSpec((1, 1, bps, ATT_HEADS, HEAD_DIM), lambda l, b, s, pt: (l, b, s, 0, 0)),
    )
    return pl.pallas_call(
        _page_mean_kernel,
        grid_spec=grid_spec,
        out_shape=jax.ShapeDtypeStruct((depth, nb, n_pages // ppb, ATT_HEADS, HEAD_DIM), F32),
        compiler_params=pltpu.CompilerParams(dimension_semantics=("arbitrary", "arbitrary", "arbitrary")),
        name="page_means",
    )(pt_flat, *([cache_kt] * PAGES_PER_STEP))


def _topk_sample_kernel(q_ref, km_ref, o_ref):
    rows, nblk = q_ref.shape[2], km_ref.shape[1]
    blane = lax.broadcasted_iota(jnp.int32, (rows, nblk), 1).astype(F32)
    olane = lax.broadcasted_iota(jnp.int32, (rows, LANES), 1)
    for hd in range(ATT_HEADS):
        qh = q_ref[hd, 0][:, 0:HEAD_DIM]
        gs = _dot3_nt(qh, km_ref[0, :, hd, :])
        _, firsts = _top3_select(gs, blane)
        out = jnp.zeros((rows, LANES), jnp.int32)
        for t, f in enumerate(firsts):
            out = jnp.where(olane == t, f.astype(jnp.int32), out)
        o_ref[0, hd] = out


def _topk_sample(q8, kmean):
    _, nb, rows, _ = q8.shape
    nblk = kmean.shape[1]
    return pl.pallas_call(
        _topk_sample_kernel,
        grid=(nb,),
        in_specs=[
            pl.BlockSpec((ATT_HEADS, 1, rows, LANES), lambda b: (0, b, 0, 0)),
            pl.BlockSpec((1, nblk, ATT_HEADS, HEAD_DIM), lambda b: (b, 0, 0, 0)),
        ],
        out_specs=pl.BlockSpec((1, ATT_HEADS, rows, LANES), lambda b: (b, 0, 0, 0)),
        out_shape=jax.ShapeDtypeStruct((nb, ATT_HEADS, rows, LANES), jnp.int32),
        compiler_params=pltpu.CompilerParams(dimension_semantics=("arbitrary",)),
        name="topk_sample",
    )(q8, kmean)


def _attn_sample_kernel(pg_ref, q_ref, kn_ref, vn_ref, *refs, dec_seq, n_sel_pages):
    del pg_ref
    kp_refs = refs[:n_sel_pages]
    vp_refs = refs[n_sel_pages:2 * n_sel_pages]
    o_ref = refs[2 * n_sel_pages]
    odd = pl.program_id(1) % 2
    q = (q_ref[0, 0][:, 0:HEAD_DIM] * (HEAD_DIM ** -0.5)).astype(BF16)
    rows = q.shape[0]
    lane = lax.broadcasted_iota(jnp.int32, (rows, LANES), 1)
    prow = lax.broadcasted_iota(jnp.int32, (rows, PAGE_SIZE), 0)
    pages_per_query = n_sel_pages // dec_seq

    ss = []
    for n in range(n_sel_pages):
        s = _dot(q, kp_refs[n][0, 0, 0].astype(BF16))
        ss.append(jnp.where(prow == n // pages_per_query, s, -jnp.inf))
    orow = lax.broadcasted_iota(jnp.int32, (rows, rows), 0)
    ocol = lax.broadcasted_iota(jnp.int32, (rows, rows), 1)
    s_own = _dot_nt(q, kn_ref[0, 0][:, 0:HEAD_DIM].astype(BF16))
    s_own = jnp.where((ocol <= orow) & (ocol < dec_seq), s_own, -jnp.inf)

    m = jnp.max(s_own, axis=1, keepdims=True)
    for s in ss:
        m = jnp.maximum(m, jnp.max(s, axis=1, keepdims=True))
    p_own = jnp.exp(s_own - m)
    den = jnp.sum(p_own, axis=1, keepdims=True)
    acc = _dot(p_own.astype(BF16), vn_ref[0, 0][:, 0:HEAD_DIM].astype(BF16))
    for n in range(n_sel_pages):
        p = jnp.exp(ss[n] - m)
        den = den + jnp.sum(p, axis=1, keepdims=True)
        acc = acc + _dot_nt(p.astype(BF16), vp_refs[n][0, 0, 0].astype(BF16))
    val = acc / den
    val2 = jnp.concatenate([val, val], axis=1)

    @pl.when(odd == 0)
    def _():
        o_ref[0] = val2

    @pl.when(odd == 1)
    def _():
        o_ref[0] = jnp.where(lane < HEAD_DIM, o_ref[0], val2)


def _attention_sample(pg_flat, q8, kn8, vn8, cache_kt, cache_vt, *, layer, nb, dec_seq):
    ppb = MOBA_BLOCK // PAGE_SIZE
    rows = q8.shape[2]
    n_sel_pages = dec_seq * MOBA_TOPK * ppb

    def page_spec(n):
        return pl.BlockSpec((1, 1, 1, HEAD_DIM, PAGE_SIZE),
                            lambda b, h, pg: (layer, pg[(b * ATT_HEADS + h) * n_sel_pages + n], h, 0, 0))

    head = pl.BlockSpec((1, 1, rows, LANES), lambda b, h, pg: (h, b, 0, 0))
    grid_spec = pltpu.PrefetchScalarGridSpec(
        num_scalar_prefetch=1,
        grid=(nb, ATT_HEADS),
        in_specs=[head, head, head] + [page_spec(n) for n in range(n_sel_pages)] * 2,
        out_specs=pl.BlockSpec((1, rows, LANES), lambda b, h, pg: (b, 0, h // 2)),
    )
    return pl.pallas_call(
        functools.partial(_attn_sample_kernel, dec_seq=dec_seq, n_sel_pages=n_sel_pages),
        grid_spec=grid_spec,
        out_shape=jax.ShapeDtypeStruct((nb, rows, ATT_WIDTH), F32),
        compiler_params=pltpu.CompilerParams(dimension_semantics=("arbitrary", "arbitrary")),
        name="attention_sample",
    )(pg_flat, q8, kn8, vn8, *([cache_kt] * n_sel_pages), *([cache_vt] * n_sel_pages))


def _rope_tables(pos):
    half = ROT_DIM // 2
    inv = jnp.exp(-math.log(ROPE_THETA) * jnp.arange(half, dtype=F32) * (2.0 / ROT_DIM))
    ang = pos.astype(F32)[:, None] * inv[None, :]
    cos, sin = jnp.cos(ang), jnp.sin(ang)
    n = pos.shape[0]
    pad = jnp.zeros((n, HEAD_DIM - ROT_DIM), F32)
    zero = jnp.zeros((n, half), F32)
    cos_h = jnp.concatenate([cos, cos, pad + 1.0], axis=1)
    s1_h = jnp.concatenate([-sin, zero, pad], axis=1)
    s2_h = jnp.concatenate([zero, sin, pad], axis=1)
    two = lambda t: jnp.concatenate([t, t], axis=1)
    return two(cos_h), two(s1_h), two(s2_h)


def _pad_seq_rows(a, nbs, dec_seq, rows):
    a = a.reshape(a.shape[0], nbs, dec_seq, a.shape[-1])
    return jnp.pad(a, ((0, 0), (0, 0), (0, rows - dec_seq), (0, 0)))


def _sample_attend(q_hm, k_hm, v_hm, kmean_l, pt_flat, cache_kt, cache_vt, *, layer, nbs, dec_seq, n_pages):
    q8 = _pad_seq_rows(q_hm, nbs, dec_seq, 8)
    k8 = _pad_seq_rows(k_hm, nbs, dec_seq, 8)
    v8 = _pad_seq_rows(v_hm, nbs, dec_seq, 8)
    idx = _topk_sample(q8, kmean_l)
    ppb = MOBA_BLOCK // PAGE_SIZE
    logical = idx[:, :, :dec_seq, :MOBA_TOPK, None] * ppb + jnp.arange(ppb, dtype=jnp.int32)
    pg_flat = jnp.take_along_axis(pt_flat.reshape(nbs, n_pages), logical.reshape(nbs, -1), axis=1).reshape(-1)
    o = _attention_sample(pg_flat, q8, k8, v8, cache_kt, cache_vt, layer=layer, nb=nbs, dec_seq=dec_seq)
    return o[:, :dec_seq].reshape(nbs * dec_seq, ATT_WIDTH)


def kernel(x_prompt, x_sample, cache_k, cache_v, state_conv, state_ffn, page_table, c_prompt, c_sample,
           norm1_g, norm2_g, w_ada, b_ada, w_in, q_norm_g, k_norm_g, conv_w, conv_b, conv_ln_g, conv_ln_b,
           attn_out_g, conv_out_g, w_out, w_fc1, ffn_conv_w, ffn_conv_b, w_fc2):
    nb, seq, _ = x_prompt.shape
    nbs, dec_seq, _ = x_sample.shape
    depth = w_in.shape[0]
    n_pages = page_table.shape[1]
    past = n_pages * PAGE_SIZE
    n_pool = cache_k.shape[1]
    n_rows_s = nbs * dec_seq
    assert past % MOBA_BLOCK == 0 and past // MOBA_BLOCK >= MOBA_TOPK
    assert seq % TM_IN == 0 and seq % TM_FFN == 0 and TM_IN % MOBA_BLOCK == 0 and n_pages % PAGES_PER_STEP == 0

    n_c = nb + nbs
    c_rows = -(-n_c // 8) * 8
    c_all = jnp.pad(jnp.concatenate([c_prompt, c_sample], axis=0), ((0, c_rows - n_c), (0, 0)))
    m_all = _modulation(c_all, w_ada, b_ada)

    w_in_bf, w_out_bf = w_in.astype(BF16), w_out.astype(BF16)
    w_fc1_bf, w_fc2_bf = w_fc1.astype(BF16), w_fc2.astype(BF16)
    head_id = jnp.arange(ATT_WIDTH) // HEAD_DIM
    bd = jnp.where(head_id[:, None] == head_id[None, :], 1.0 / HEAD_DIM, 0.0).astype(BF16)

    tabs_p = _rope_tables(jnp.arange(seq))
    tabs_s = _rope_tables(past + (jnp.arange(n_rows_s) % dec_seq))

    cache_kt = cache_k.transpose(0, 1, 3, 4, 2)
    cache_vt = cache_v.transpose(0, 1, 3, 4, 2)
    pt_flat = page_table.reshape(-1)
    kmean_s = _page_means(cache_kt, pt_flat, nb=nbs, n_pages=n_pages)

    xp = x_prompt.reshape(nb * seq, D_MODEL)
    xs = x_sample.reshape(n_rows_s, D_MODEL)
    outs = [[] for _ in range(8)]
    for l in range(depth):
        row = lambda a: a[l].reshape(1, -1)
        mp = m_all[l, :nb].reshape(nb, N_MOD, 1, D_MODEL)
        ms = jnp.repeat(m_all[l, nb:n_c].reshape(nbs, N_MOD, D_MODEL), dec_seq, axis=0)
        mp = [mp[:, i] for i in range(N_MOD)]
        ms = [ms[:, i] for i in range(N_MOD)]
        qg = jnp.tile(q_norm_g[l], ATT_HEADS).reshape(1, -1)
        kg = jnp.tile(k_norm_g[l], ATT_HEADS).reshape(1, -1)

        k, v, u, kaug, vaug, qaug = _mixer_in(
            xp, mp[0], mp[1], row(norm1_g), w_in_bf[l], bd, qg, kg, *tabs_p, prompt=True, tm=TM_IN, seq=seq)
        att = _attention_prompt(qaug, kaug, vaug, nb=nb, seq=seq)
        cvn = _conv_prompt(u, conv_w[l], row(conv_b), row(conv_ln_g), row(conv_ln_b), row(conv_out_g),
                           nb=nb, seq=seq)
        xp_new, ffn_state_p = _ffn_prompt(
            xp, att.reshape(nb * seq, ATT_WIDTH), cvn.reshape(nb * seq, CONV_WIDTH),
            mp[2], mp[3], mp[4], mp[5], row(attn_out_g), row(norm2_g),
            w_out_bf[l], w_fc1_bf[l], ffn_conv_w[l], row(ffn_conv_b), w_fc2_bf[l], nb=nb, seq=seq)
        outs[0].append(k.reshape(nb, seq, ATT_HEADS, HEAD_DIM))
        outs[1].append(v.reshape(nb, seq, ATT_HEADS, HEAD_DIM))
        outs[2].append(u.reshape(nb, seq, CONV_WIDTH)[:, seq - (CONV_K - 1):])
        outs[3].append(ffn_state_p)
        xp = xp_new

        ks, vs, us, q_hm, k_hm, v_hm = _mixer_in(
            xs, ms[0][None], ms[1][None], row(norm1_g), w_in_bf[l], bd, qg, kg, *tabs_s,
            prompt=False, tm=n_rows_s, seq=n_rows_s)
        att_s = _sample_attend(q_hm, k_hm, v_hm, kmean_s[l], pt_flat, cache_kt, cache_vt,
                               layer=l, nbs=nbs, dec_seq=dec_seq, n_pages=n_pages)
        u_ext = jnp.concatenate([state_conv[l], us.reshape(nbs, dec_seq, CONV_WIDTH)], axis=1)
        cvn_s = _conv_sample(u_ext, conv_w[l], row(conv_b), row(conv_ln_g), row(conv_ln_b), row(conv_out_g))
        st = state_ffn[l]
        pm1 = jnp.repeat(st[:, 1], dec_seq, axis=0)
        pm2 = jnp.stack([st[:, 0], st[:, 1]] + [st[:, 1]] * (dec_seq - 2), axis=1).reshape(n_rows_s, D_FF)
        xs_new, g_s = _ffn_sample(
            xs, att_s, cvn_s.reshape(n_rows_s, CONV_WIDTH), ms[2], ms[3], ms[4], ms[5],
            row(attn_out_g), row(norm2_g), w_out_bf[l], w_fc1_bf[l], ffn_conv_w[l], row(ffn_conv_b),
            w_fc2_bf[l], pm1, pm2, seq=dec_seq)
        outs[4].append(ks.reshape(nbs, dec_seq, ATT_HEADS, HEAD_DIM))
        outs[5].append(vs.reshape(nbs, dec_seq, ATT_HEADS, HEAD_DIM))
        outs[6].append(u_ext[:, dec_seq:])
        outs[7].append(g_s.reshape(nbs, dec_seq, D_FF)[:, dec_seq - (FFN_CONV_K - 1):])
        xs = xs_new

    stk = [jnp.stack(o) for o in outs]
    return (xp.reshape(nb, seq, D_MODEL), xs.reshape(nbs, dec_seq, D_MODEL),
            stk[0], stk[1], stk[2], stk[3], stk[4], stk[5], stk[6], stk[7])
```

```python
import functools
import math

import jax
import jax.numpy as jnp
from jax import lax
from jax.experimental import pallas as pl
from jax.experimental.pallas import tpu as pltpu

F32 = jnp.float32
BF16 = jnp.bfloat16

D_MODEL = 1024
HEAD_DIM = 64
ATT_HEADS = 8
ATT_WIDTH = 512
CONV_WIDTH = 512
CONV_K = 31
ROT_DIM = 16
ROPE_THETA = 500000.0
MOBA_BLOCK = 256
MOBA_TOPK = 3
PAGE_SIZE = 128
D_FF = 2816
FFN_CONV_K = 3
N_MOD = 6
EPS = 1e-6
IN_WIDTH = 3 * ATT_WIDTH + 2 * CONV_WIDTH

LANES = 128
SUBLANES = 8
NEG_BIG = -1e30
FF_CHUNK = 256
N_FF_CHUNKS = D_FF // FF_CHUNK
PAGES_PER_STEP = 32

TM_IN = 512
TM_FFN = 512
TC_CONV = 256
CONV_HALO = 32
CONV_ROWS = 64
ATT_KEY_BLOCKS = 4


def _dot(a, b):
    return jnp.dot(a, b, preferred_element_type=F32)


def _dot_nt(a, b):
    return lax.dot_general(a, b, (((1,), (1,)), ((), ())), preferred_element_type=F32)


def _split(a):
    hi = a.astype(BF16)
    lo = (a - hi.astype(F32)).astype(BF16)
    return hi, lo


def _dot3(a, b):
    ah, al = _split(a)
    bh, bl = _split(b)
    return _dot(ah, bh) + _dot(al, bh) + _dot(ah, bl)


def _dot3_nt(a, b):
    ah, al = _split(a)
    bh, bl = _split(b)
    return _dot_nt(ah, bh) + _dot_nt(al, bh) + _dot_nt(ah, bl)


def _sigmoid(x):
    return 1.0 / (1.0 + jnp.exp(-x))


def _silu(x):
    return x * _sigmoid(x)


def _rms(x, g):
    ms = jnp.mean(x * x, axis=-1, keepdims=True)
    return x * lax.rsqrt(ms + EPS) * g


def _mod_kernel(c_ref, w_ref, b_ref, o_ref):
    a = _silu(c_ref[...])
    o_ref[0] = _dot3(a, w_ref[0]) + b_ref[0]


def _modulation(c_all, w_ada, b_ada):
    depth = w_ada.shape[0]
    rows = c_all.shape[0]
    nt = (N_MOD * D_MODEL) // D_MODEL
    return pl.pallas_call(
        _mod_kernel,
        grid=(depth, nt),
        in_specs=[
            pl.BlockSpec((rows, D_MODEL), lambda l, n: (0, 0)),
            pl.BlockSpec((1, D_MODEL, D_MODEL), lambda l, n: (l, 0, n)),
            pl.BlockSpec((1, 1, D_MODEL), lambda l, n: (l, 0, n)),
        ],
        out_specs=pl.BlockSpec((1, rows, D_MODEL), lambda l, n: (l, 0, n)),
        out_shape=jax.ShapeDtypeStruct((depth, rows, N_MOD * D_MODEL), F32),
        name="modulation",
    )(c_all, w_ada, b_ada.reshape(depth, 1, N_MOD * D_MODEL))


def _tile_lanes(t, n):
    return jnp.concatenate([t] * n, axis=1)


def _mixer_in_kernel(x_ref, sh_ref, sc_ref, ng_ref, w_ref, bd_ref, qg_ref, kg_ref,
                     cos_ref, s1_ref, s2_ref, k_ref, v_ref, u_ref, *extra,
                     prompt, tm, tiles_per_seq):
    x = x_ref[...]
    h = _rms(x, ng_ref[...])
    h = h * (1.0 + sc_ref[0]) + sh_ref[0]
    z = _dot(h.astype(BF16), w_ref[...])

    reps = ATT_WIDTH // LANES
    cos = _tile_lanes(cos_ref[...], reps)
    s1 = _tile_lanes(s1_ref[...], reps)
    s2 = _tile_lanes(s2_ref[...], reps)
    bd = bd_ref[...]

    def head_norm_rope(t, g):
        hi, lo = _split(t * t)
        ms = _dot(hi, bd) + _dot(lo, bd)
        tn = t * lax.rsqrt(ms + EPS) * g
        return (tn * cos + pltpu.roll(tn, ATT_WIDTH - ROT_DIM // 2, 1) * s1
                + pltpu.roll(tn, ROT_DIM // 2, 1) * s2)

    q = head_norm_rope(z[:, 0:ATT_WIDTH], qg_ref[...])
    k = head_norm_rope(z[:, ATT_WIDTH:2 * ATT_WIDTH], kg_ref[...])
    v = z[:, 2 * ATT_WIDTH:3 * ATT_WIDTH]
    a = z[:, 3 * ATT_WIDTH:3 * ATT_WIDTH + CONV_WIDTH]
    g = z[:, 3 * ATT_WIDTH + CONV_WIDTH:]
    if prompt:
        k_ref[0] = k.T.reshape(ATT_HEADS, HEAD_DIM, tm)
        v_ref[0] = v.T.reshape(ATT_HEADS, HEAD_DIM, tm)
    else:
        k_ref[...] = k
        v_ref[...] = v
    u_ref[...] = a * _sigmoid(g)

    if prompt:
        ka_ref, va_ref, qa_ref, km_ref = extra
        nblk = km_ref.shape[0]
        t_in_seq = pl.program_id(0) % tiles_per_seq
        lane = lax.broadcasted_iota(jnp.int32, (tm, LANES), 1)
        row = lax.broadcasted_iota(jnp.int32, (tm, LANES), 0)
        blk = (t_in_seq * tm + row) // MOBA_BLOCK
        onehot = jnp.where(lane - HEAD_DIM == blk, 1.0, 0.0)

        @pl.when(t_in_seq == 0)
        def _():
            km_ref[...] = jnp.zeros(km_ref.shape, F32)

        bpt = tm // MOBA_BLOCK
        for c in range(bpt):
            km_ref[pl.ds(t_in_seq * bpt + c, 1), :] = (
                jnp.sum(k[c * MOBA_BLOCK:(c + 1) * MOBA_BLOCK], axis=0, keepdims=True) * (1.0 / MOBA_BLOCK))
        km = km_ref[...]

        kblk = lax.broadcasted_iota(jnp.int32, (nblk, tm), 0)
        kblkf = kblk.astype(F32)
        qblk = (t_in_seq * tm + lax.broadcasted_iota(jnp.int32, (nblk, tm), 1)) // MOBA_BLOCK
        past = kblk < qblk
        klane = lax.broadcasted_iota(jnp.int32, (nblk, LANES), 1)
        place = jnp.where(klane == lax.broadcasted_iota(jnp.int32, (nblk, LANES), 0) + HEAD_DIM,
                          1.0, 0.0).astype(BF16)
        for hd in range(ATT_HEADS):
            lo_l = LANES * (hd // 2)
            qt = q[:, lo_l:lo_l + LANES]
            kt = k[:, lo_l:lo_l + LANES]
            vt = v[:, lo_l:lo_l + LANES]
            kmt = km[:, lo_l:lo_l + LANES]
            if hd % 2 == 1:
                qt = pltpu.roll(qt, HEAD_DIM, 1)
                kt = pltpu.roll(kt, HEAD_DIM, 1)
                vt = pltpu.roll(vt, HEAD_DIM, 1)
                kmt = pltpu.roll(kmt, HEAD_DIM, 1)
            ka_ref[0, hd] = jnp.where(lane < HEAD_DIM, kt, onehot).astype(BF16)
            va_ref[0, hd] = jnp.where(lane < HEAD_DIM, vt, 1.0).astype(BF16)
            gs = _dot3_nt(jnp.where(klane < HEAD_DIM, kmt, 0.0), qt)
            selm, _ = _top3_select(jnp.where(past, gs, -jnp.inf), kblkf, axis=0)
            bias_t = jnp.where(past & (selm == 0.0), NEG_BIG, 0.0).astype(BF16)
            bias = lax.dot_general(bias_t, place, (((0,), (0,)), ((), ())), preferred_element_type=F32)
            qa_ref[0, hd] = jnp.where(lane < HEAD_DIM, qt * (HEAD_DIM ** -0.5), bias).astype(BF16)
    else:
        lane = lax.broadcasted_iota(jnp.int32, (tm, LANES), 1)
        for src, dst in zip((q, k, v), extra):
            for hd in range(ATT_HEADS):
                lo_l = LANES * (hd // 2)
                t = src[:, lo_l:lo_l + LANES]
                if hd % 2 == 1:
                    t = pltpu.roll(t, HEAD_DIM, 1)
                dst[hd] = jnp.where(lane < HEAD_DIM, t, 0.0)


def _mixer_in(x, shift, scale, norm_g, w_in_bf, bd, qg, kg, cos_t, s1_t, s2_t, *, prompt, tm, seq):
    n = x.shape[0]
    nt = n // tm
    tiles_per_seq = seq // tm
    mod_rows = shift.shape[1]
    if prompt:
        mod_map = lambda t: (t // tiles_per_seq, 0, 0)
        tab_map = lambda t: (t % tiles_per_seq, 0)
    else:
        mod_map = lambda t: (t, 0, 0)
        tab_map = lambda t: (t, 0)
    const = lambda t: (0, 0)
    row_map = lambda t: (t, 0)
    in_specs = [
        pl.BlockSpec((tm, D_MODEL), row_map),
        pl.BlockSpec((1, mod_rows, D_MODEL), mod_map),
        pl.BlockSpec((1, mod_rows, D_MODEL), mod_map),
        pl.BlockSpec((1, D_MODEL), const),
        pl.BlockSpec((D_MODEL, IN_WIDTH), const),
        pl.BlockSpec((ATT_WIDTH, ATT_WIDTH), const),
        pl.BlockSpec((1, ATT_WIDTH), const),
        pl.BlockSpec((1, ATT_WIDTH), const),
        pl.BlockSpec((tm, LANES), tab_map),
        pl.BlockSpec((tm, LANES), tab_map),
        pl.BlockSpec((tm, LANES), tab_map),
    ]
    out_specs = [pl.BlockSpec((tm, ATT_WIDTH), row_map)] * 3
    out_shape = [jax.ShapeDtypeStruct((n, ATT_WIDTH), F32)] * 3
    scratch = []
    if prompt:
        nb = n // seq
        aug_map = lambda t: (t // tiles_per_seq, 0, t % tiles_per_seq, 0)
        kvt_spec = pl.BlockSpec((1, ATT_HEADS, HEAD_DIM, tm), lambda t: (t // tiles_per_seq, 0, 0, t % tiles_per_seq))
        out_specs[0:2] = [kvt_spec] * 2
        out_shape[0:2] = [jax.ShapeDtypeStruct((nb, ATT_HEADS, HEAD_DIM, seq), F32)] * 2
        out_specs += [pl.BlockSpec((1, ATT_HEADS, tm, LANES), aug_map)] * 3
        out_shape += [jax.ShapeDtypeStruct((nb, ATT_HEADS, seq, LANES), BF16)] * 3
        scratch = [pltpu.VMEM((seq // MOBA_BLOCK, ATT_WIDTH), F32)]
    else:
        out_specs += [pl.BlockSpec((ATT_HEADS, tm, LANES), lambda t: (0, t, 0))] * 3
        out_shape += [jax.ShapeDtypeStruct((ATT_HEADS, n, LANES), F32)] * 3
    return pl.pallas_call(
        functools.partial(_mixer_in_kernel, prompt=prompt, tm=tm, tiles_per_seq=tiles_per_seq),
        grid=(nt,),
        in_specs=in_specs,
        out_specs=out_specs,
        out_shape=out_shape,
        scratch_shapes=scratch,
        compiler_params=pltpu.CompilerParams(dimension_semantics=("arbitrary",)),
        name="mixer_in_prompt" if prompt else "mixer_in_sample",
    )(x, shift, scale, norm_g, w_in_bf, bd, qg, kg, cos_t, s1_t, s2_t)


def _top3_select(cur, lanef, axis=1):
    selm = jnp.zeros(cur.shape, F32)
    firsts = []
    for _ in range(MOBA_TOPK):
        m = jnp.max(cur, axis=axis, keepdims=True)
        first = jnp.min(jnp.where(cur == m, lanef, 1e9), axis=axis, keepdims=True)
        pick = (lanef == first) & (m > -jnp.inf)
        selm = jnp.where(pick, 1.0, selm)
        cur = jnp.where(pick, -jnp.inf, cur)
        firsts.append(first)
    return selm, firsts


def _attn_kernel(qa_ref, ka_ref, va_ref, o_ref, *, kb):
    i = pl.program_id(2)
    tq = MOBA_BLOCK
    lane = lax.broadcasted_iota(jnp.int32, (tq, LANES), 1)
    gw = kb * tq
    g_own = i // kb
    row = lax.broadcasted_iota(jnp.int32, (tq, gw), 0)
    col = lax.broadcasted_iota(jnp.int32, (tq, gw), 1)
    causal = col - row <= (i - g_own * kb) * tq
    qas = [qa_ref[0, 0], qa_ref[0, 1]]

    def group_rows(g):
        return pl.ds(pl.multiple_of(g * gw, gw), gw)

    init = []
    for hh in range(2):
        s = _dot_nt(qas[hh], ka_ref[0, hh, group_rows(g_own), :])
        s = jnp.where(causal, s, -jnp.inf)
        m = jnp.max(s, axis=1, keepdims=True)
        p = jnp.exp(s - m)
        acc = _dot(p.astype(BF16), va_ref[0, hh, group_rows(g_own), :])
        init.append((m, acc))

    def body(g, carry):
        new = []
        for hh in range(2):
            m, acc = carry[hh]
            s = _dot_nt(qas[hh], ka_ref[0, hh, group_rows(g), :])
            mn = jnp.maximum(m, jnp.max(s, axis=1, keepdims=True))
            p = jnp.exp(s - mn)
            acc = acc * jnp.exp(m - mn) + _dot(p.astype(BF16), va_ref[0, hh, group_rows(g), :])
            new.append((mn, acc))
        return tuple(new)

    (_, acc0), (_, acc1) = lax.fori_loop(0, g_own, body, tuple(init))
    o0 = acc0 / pltpu.roll(acc0, HEAD_DIM, 1)
    o1 = pltpu.roll(acc1, HEAD_DIM, 1) / acc1
    o_ref[0] = jnp.where(lane < HEAD_DIM, o0, o1)


def _attention_prompt(qaug, kaug, vaug, *, nb, seq):
    nq = seq // MOBA_BLOCK
    hp = ATT_HEADS // 2
    kb = math.gcd(ATT_KEY_BLOCKS, nq)
    return pl.pallas_call(
        functools.partial(_attn_kernel, kb=kb),
        grid=(nb, hp, nq),
        in_specs=[
            pl.BlockSpec((1, 2, MOBA_BLOCK, LANES), lambda b, h, i: (b, h, i, 0)),
            pl.BlockSpec((1, 2, seq, LANES), lambda b, h, i: (b, h, 0, 0)),
            pl.BlockSpec((1, 2, seq, LANES), lambda b, h, i: (b, h, 0, 0)),
        ],
        out_specs=pl.BlockSpec((1, MOBA_BLOCK, LANES), lambda b, h, i: (b, i, h)),
        out_shape=jax.ShapeDtypeStruct((nb, seq, ATT_WIDTH), F32),
        compiler_params=pltpu.CompilerParams(dimension_semantics=("arbitrary", "arbitrary", "arbitrary")),
        name="attention_prompt",
    )(qaug, kaug, vaug)


def _conv_post(y, lg, lb, cg):
    mu = jnp.mean(y, axis=-1, keepdims=True)
    yc = y - mu
    var = jnp.mean(yc * yc, axis=-1, keepdims=True)
    cv = _silu(yc * lax.rsqrt(var + EPS) * lg + lb)
    return _rms(cv, cg)


def _conv_kernel(prev_ref, cur_ref, w_ref, b_ref, lg_ref, lb_ref, cg_ref, o_ref, ext_ref, sh_ref, *, tc):
    t = pl.program_id(1)
    n_ext = CONV_HALO + tc
    ext_ref[0:CONV_HALO, :] = jnp.where(t > 0, prev_ref[0], 0.0)
    ext_ref[CONV_HALO:n_ext, :] = cur_ref[0]
    ext_ref[n_ext:n_ext + SUBLANES, :] = jnp.zeros((SUBLANES, CONV_WIDTH), F32)
    for s in range(1, SUBLANES):
        sh_ref[s] = ext_ref[pl.ds(s, n_ext), :]
    off = CONV_HALO - (CONV_K - 1)
    for c in range(tc // CONV_ROWS):
        acc = jnp.zeros((CONV_ROWS, CONV_WIDTH), F32)
        for j in range(CONV_K):
            s = (off + j) % SUBLANES
            rows = pl.ds(c * CONV_ROWS + off + j - s, CONV_ROWS)
            win = ext_ref[rows, :] if s == 0 else sh_ref[s, rows, :]
            acc = acc + w_ref[j:j + 1, :] * win
        y = acc + b_ref[...]
        o_ref[0, c * CONV_ROWS:(c + 1) * CONV_ROWS, :] = _conv_post(
            y, lg_ref[...], lb_ref[...], cg_ref[...]).astype(o_ref.dtype)


def _conv_prompt(u, conv_w, conv_b, ln_g, ln_b, cg, *, nb, seq):
    tc = TC_CONV
    nt = seq // tc
    hpt = tc // CONV_HALO
    vec = pl.BlockSpec((1, CONV_WIDTH), lambda b, t: (0, 0))
    u3 = u.reshape(nb, seq, CONV_WIDTH)
    return pl.pallas_call(
        functools.partial(_conv_kernel, tc=tc),
        grid=(nb, nt),
        in_specs=[
            pl.BlockSpec((1, CONV_HALO, CONV_WIDTH), lambda b, t: (b, jnp.maximum(t * hpt - 1, 0), 0)),
            pl.BlockSpec((1, tc, CONV_WIDTH), lambda b, t: (b, t, 0)),
            pl.BlockSpec((CONV_K, CONV_WIDTH), lambda b, t: (0, 0)),
            vec, vec, vec, vec,
        ],
        out_specs=pl.BlockSpec((1, tc, CONV_WIDTH), lambda b, t: (b, t, 0)),
        out_shape=jax.ShapeDtypeStruct((nb, seq, CONV_WIDTH), BF16),
        scratch_shapes=[pltpu.VMEM((CONV_HALO + tc + SUBLANES, CONV_WIDTH), F32),
                        pltpu.VMEM((SUBLANES, CONV_HALO + tc, CONV_WIDTH), F32)],
        compiler_params=pltpu.CompilerParams(dimension_semantics=("arbitrary", "arbitrary")),
        name="conv_prompt",
    )(u3, u3, conv_w, conv_b, ln_g, ln_b, cg)


def _conv_sample_kernel(ue_ref, w_ref, b_ref, lg_ref, lb_ref, cg_ref, o_ref, *, rows):
    ue = ue_ref[0]
    w = w_ref[...]
    ys = [jnp.sum(ue[t:t + CONV_K, :] * w, axis=0, keepdims=True) for t in range(rows)]
    y = jnp.concatenate(ys, axis=0) + b_ref[...]
    o_ref[0] = _conv_post(y, lg_ref[...], lb_ref[...], cg_ref[...]).astype(o_ref.dtype)


def _conv_sample(u_ext, conv_w, conv_b, ln_g, ln_b, cg):
    nb, ext_rows, _ = u_ext.shape
    rows = ext_rows - (CONV_K - 1)
    vec = pl.BlockSpec((1, CONV_WIDTH), lambda b: (0, 0))
    return pl.pallas_call(
        functools.partial(_conv_sample_kernel, rows=rows),
        grid=(nb,),
        in_specs=[
            pl.BlockSpec((1, ext_rows, CONV_WIDTH), lambda b: (b, 0, 0)),
            pl.BlockSpec((CONV_K, CONV_WIDTH), lambda b: (0, 0)),
            vec, vec, vec, vec,
        ],
        out_specs=pl.BlockSpec((1, rows, CONV_WIDTH), lambda b: (b, 0, 0)),
        out_shape=jax.ShapeDtypeStruct((nb, rows, CONV_WIDTH), F32),
        compiler_params=pltpu.CompilerParams(dimension_semantics=("arbitrary",)),
        name="conv_sample",
    )(u_ext, conv_w, conv_b, ln_g, ln_b, cg)


def _merge(x, att, cvn, g1, ag, wo_ref):
    attn = _rms(att, ag).astype(BF16)
    merged = _dot(attn, wo_ref[0:ATT_WIDTH, :]) + _dot(cvn.astype(BF16), wo_ref[ATT_WIDTH:, :])
    return x + g1 * merged


def _ffn_prompt_kernel(x_ref, att_ref, cvn_ref, g1_ref, sh_ref, sc_ref, g2_ref, ag_ref, ng_ref,
                       wo_ref, w1_ref, cw_ref, cb_ref, w2_ref, o_ref, st_ref, carry_ref, act_ref,
                       *, tm, tiles_per_seq):
    t_in_seq = pl.program_id(0) % tiles_per_seq

    @pl.when(t_in_seq == 0)
    def _():
        carry_ref[...] = jnp.zeros(carry_ref.shape, F32)

    x1 = _merge(x_ref[...], att_ref[...], cvn_ref[...], g1_ref[0], ag_ref[...], wo_ref)
    h2 = (_rms(x1, ng_ref[...]) * (1.0 + sc_ref[0]) + sh_ref[0]).astype(BF16)
    row = lax.broadcasted_iota(jnp.int32, (tm, FF_CHUNK), 0)
    for c in range(N_FF_CHUNKS):
        cols = slice(c * FF_CHUNK, (c + 1) * FF_CHUNK)
        g = _dot(h2, w1_ref[:, cols])
        up = _dot(h2, w1_ref[:, D_FF + c * FF_CHUNK:D_FF + (c + 1) * FF_CHUNK])
        p0 = carry_ref[0:1, cols]
        p1 = carry_ref[1:2, cols]
        gm1 = jnp.where(row == 0, p1, pltpu.roll(g, 1, 0))
        gm2 = jnp.where(row == 0, p0, jnp.where(row == 1, p1, pltpu.roll(g, 2, 0)))
        y = cw_ref[0:1, cols] * gm2 + cw_ref[1:2, cols] * gm1 + cw_ref[2:3, cols] * g + cb_ref[:, cols]
        act_ref[:, cols] = (_silu(y) * up).astype(BF16)
        tail = g[tm - (FFN_CONV_K - 1):tm, :]
        carry_ref[0:FFN_CONV_K - 1, cols] = tail
        st_ref[0, :, cols] = tail
    o_ref[...] = x1 + g2_ref[0] * _dot(act_ref[...], w2_ref[...])


def _ffn_prompt(x, att, cvn, gate1, shift2, scale2, gate2, att_g, norm2_g, wo_bf, w1_bf, cw, cb, w2_bf,
                *, nb, seq):
    tm = TM_FFN
    n = x.shape[0]
    nt = n // tm
    tiles_per_seq = seq // tm
    row_map = lambda t: (t, 0)
    mod_map = lambda t: (t // tiles_per_seq, 0, 0)
    const = lambda t: (0, 0)
    mod = pl.BlockSpec((1, 1, D_MODEL), mod_map)

    def resident(shape):
        return pl.BlockSpec(shape, const, pipeline_mode=pl.Buffered(1))

    return pl.pallas_call(
        functools.partial(_ffn_prompt_kernel, tm=tm, tiles_per_seq=tiles_per_seq),
        grid=(nt,),
        in_specs=[
            pl.BlockSpec((tm, D_MODEL), row_map),
            pl.BlockSpec((tm, ATT_WIDTH), row_map),
            pl.BlockSpec((tm, CONV_WIDTH), row_map),
            mod, mod, mod, mod,
            pl.BlockSpec((1, ATT_WIDTH), const),
            pl.BlockSpec((1, D_MODEL), const),
            resident((D_MODEL, D_MODEL)),
            resident((D_MODEL, 2 * D_FF)),
            pl.BlockSpec((FFN_CONV_K, D_FF), const),
            pl.BlockSpec((1, D_FF), const),
            resident((D_FF, D_MODEL)),
        ],
        out_specs=[
            pl.BlockSpec((tm, D_MODEL), row_map),
            pl.BlockSpec((1, FFN_CONV_K - 1, D_FF), mod_map),
        ],
        out_shape=[
            jax.ShapeDtypeStruct((n, D_MODEL), F32),
            jax.ShapeDtypeStruct((nb, FFN_CONV_K - 1, D_FF), F32),
        ],
        scratch_shapes=[
            pltpu.VMEM((8, D_FF), F32),
            pltpu.VMEM((tm, D_FF), BF16),
        ],
        compiler_params=pltpu.CompilerParams(dimension_semantics=("arbitrary",)),
        name="ffn_prompt",
    )(x, att, cvn, gate1, shift2, scale2, gate2, att_g, norm2_g, wo_bf, w1_bf, cw, cb, w2_bf)


def _ffn_sample_kernel(x_ref, att_ref, cvn_ref, g1_ref, sh_ref, sc_ref, g2_ref, ag_ref, ng_ref,
                       wo_ref, w1g_ref, w1u_ref, cw_ref, cb_ref, w2_ref, pm1_ref, pm2_ref,
                       o_ref, gout_ref, x1_ref, h2_ref, acc_ref, *, rows, seq):
    c = pl.program_id(0)

    @pl.when(c == 0)
    def _():
        x1 = _merge(x_ref[...], att_ref[...], cvn_ref[...], g1_ref[...], ag_ref[...], wo_ref)
        x1_ref[...] = x1
        h2_ref[...] = (_rms(x1, ng_ref[...]) * (1.0 + sc_ref[...]) + sh_ref[...]).astype(BF16)
        acc_ref[...] = jnp.zeros(acc_ref.shape, F32)

    h2 = h2_ref[...]
    g = _dot(h2, w1g_ref[...])
    up = _dot(h2, w1u_ref[...])
    t = lax.broadcasted_iota(jnp.int32, (rows, FF_CHUNK), 0) % seq
    gm1 = jnp.where(t == 0, pm1_ref[...], pltpu.roll(g, 1, 0))
    gm2 = jnp.where(t < 2, pm2_ref[...], pltpu.roll(g, 2, 0))
    y = cw_ref[0:1, :] * gm2 + cw_ref[1:2, :] * gm1 + cw_ref[2:3, :] * g + cb_ref[...]
    act = (_silu(y) * up).astype(BF16)
    acc_ref[...] += _dot(act, w2_ref[...])
    gout_ref[...] = g

    @pl.when(c == pl.num_programs(0) - 1)
    def _():
        o_ref[...] = x1_ref[...] + g2_ref[...] * acc_ref[...]


def _ffn_sample(x, att, cvn, gate1, shift2, scale2, gate2, att_g, norm2_g, wo_bf, w1_bf, cw, cb, w2_bf,
                pm1, pm2, *, seq):
    rows = x.shape[0]
    const = lambda c: (0, 0)
    full = lambda w: pl.BlockSpec((rows, w), const)
    chunk = lambda c: (0, c)
    return pl.pallas_call(
        functools.partial(_ffn_sample_kernel, rows=rows, seq=seq),
        grid=(N_FF_CHUNKS,),
        in_specs=[
            full(D_MODEL), full(ATT_WIDTH), full(CONV_WIDTH),
            full(D_MODEL), full(D_MODEL), full(D_MODEL), full(D_MODEL),
            pl.BlockSpec((1, ATT_WIDTH), const),
            pl.BlockSpec((1, D_MODEL), const),
            pl.BlockSpec((D_MODEL, D_MODEL), const),
            pl.BlockSpec((D_MODEL, FF_CHUNK), chunk),
            pl.BlockSpec((D_MODEL, FF_CHUNK), lambda c: (0, N_FF_CHUNKS + c)),
            pl.BlockSpec((FFN_CONV_K, FF_CHUNK), chunk),
            pl.BlockSpec((1, FF_CHUNK), chunk),
            pl.BlockSpec((FF_CHUNK, D_MODEL), lambda c: (c, 0)),
            pl.BlockSpec((rows, FF_CHUNK), chunk),
            pl.BlockSpec((rows, FF_CHUNK), chunk),
        ],
        out_specs=[
            pl.BlockSpec((rows, D_MODEL), const),
            pl.BlockSpec((rows, FF_CHUNK), chunk),
        ],
        out_shape=[
            jax.ShapeDtypeStruct((rows, D_MODEL), F32),
            jax.ShapeDtypeStruct((rows, D_FF), F32),
        ],
        scratch_shapes=[
            pltpu.VMEM((rows, D_MODEL), F32),
            pltpu.VMEM((rows, D_MODEL), BF16),
            pltpu.VMEM((rows, D_MODEL), F32),
        ],
        compiler_params=pltpu.CompilerParams(dimension_semantics=("arbitrary",)),
        name="ffn_sample",
    )(x, att, cvn, gate1, shift2, scale2, gate2, att_g, norm2_g, wo_bf, w1_bf, w1_bf, cw, cb, w2_bf, pm1, pm2)


def _page_mean_kernel(pt_ref, *refs):
    del pt_ref
    page_refs, o_ref = refs[:-1], refs[-1]
    ppb = MOBA_BLOCK // PAGE_SIZE
    for n in range(len(page_refs) // ppb):
        tot = page_refs[ppb * n][0, 0]
        for p in range(1, ppb):
            tot = tot + page_refs[ppb * n + p][0, 0]
        o_ref[0, 0, n] = jnp.sum(tot, axis=-1) * (1.0 / MOBA_BLOCK)


def _page_means(cache_kt, pt_flat, *, nb, n_pages):
    depth = cache_kt.shape[0]
    ppb = MOBA_BLOCK // PAGE_SIZE
    steps = n_pages // PAGES_PER_STEP
    bps = PAGES_PER_STEP // ppb

    def page_spec(p):
        return pl.BlockSpec((1, 1, ATT_HEADS, HEAD_DIM, PAGE_SIZE),
                            lambda l, b, s, pt: (l, pt[b * n_pages + s * PAGES_PER_STEP + p], 0, 0, 0))

    grid_spec = pltpu.PrefetchScalarGridSpec(
        num_scalar_prefetch=1,
        grid=(depth, nb, steps),
        in_specs=[page_spec(p) for p in range(PAGES_PER_STEP)],
        out_specs=pl.BlockSpec((1, 1, bps, ATT_HEADS, HEAD_DIM), lambda l, b, s, pt: (l, b, s, 0, 0)),
    )
    return pl.pallas_call(
        _page_mean_kernel,
        grid_spec=grid_spec,
        out_shape=jax.ShapeDtypeStruct((depth, nb, n_pages // ppb, ATT_HEADS, HEAD_DIM), F32),
        compiler_params=pltpu.CompilerParams(dimension_semantics=("arbitrary", "arbitrary", "arbitrary")),
        name="page_means",
    )(pt_flat, *([cache_kt] * PAGES_PER_STEP))


def _topk_sample_kernel(q_ref, km_ref, o_ref):
    rows, nblk = q_ref.shape[2], km_ref.shape[1]
    blane = lax.broadcasted_iota(jnp.int32, (rows, nblk), 1).astype(F32)
    olane = lax.broadcasted_iota(jnp.int32, (rows, LANES), 1)
    for hd in range(ATT_HEADS):
        qh = q_ref[hd, 0][:, 0:HEAD_DIM]
        gs = _dot3_nt(qh, km_ref[0, :, hd, :])
        _, firsts = _top3_select(gs, blane)
        out = jnp.zeros((rows, LANES), jnp.int32)
        for t, f in enumerate(firsts):
            out = jnp.where(olane == t, f.astype(jnp.int32), out)
        o_ref[0, hd] = out


def _topk_sample(q8, kmean):
    _, nb, rows, _ = q8.shape
    nblk = kmean.shape[1]
    return pl.pallas_call(
        _topk_sample_kernel,
        grid=(nb,),
        in_specs=[
            pl.BlockSpec((ATT_HEADS, 1, rows, LANES), lambda b: (0, b, 0, 0)),
            pl.BlockSpec((1, nblk, ATT_HEADS, HEAD_DIM), lambda b: (b, 0, 0, 0)),
        ],
        out_specs=pl.BlockSpec((1, ATT_HEADS, rows, LANES), lambda b: (b, 0, 0, 0)),
        out_shape=jax.ShapeDtypeStruct((nb, ATT_HEADS, rows, LANES), jnp.int32),
        compiler_params=pltpu.CompilerParams(dimension_semantics=("arbitrary",)),
        name="topk_sample",
    )(q8, kmean)


def _attn_sample_kernel(pg_ref, q_ref, kn_ref, vn_ref, *refs, dec_seq, n_sel_pages):
    del pg_ref
    kp_refs = refs[:n_sel_pages]
    vp_refs = refs[n_sel_pages:2 * n_sel_pages]
    o_ref = refs[2 * n_sel_pages]
    odd = pl.program_id(1) % 2
    q = (q_ref[0, 0][:, 0:HEAD_DIM] * (HEAD_DIM ** -0.5)).astype(BF16)
    rows = q.shape[0]
    lane = lax.broadcasted_iota(jnp.int32, (rows, LANES), 1)
    prow = lax.broadcasted_iota(jnp.int32, (rows, PAGE_SIZE), 0)
    pages_per_query = n_sel_pages // dec_seq

    ss = []
    for n in range(n_sel_pages):
        s = _dot(q, kp_refs[n][0, 0, 0].astype(BF16))
        ss.append(jnp.where(prow == n // pages_per_query, s, -jnp.inf))
    orow = lax.broadcasted_iota(jnp.int32, (rows, rows), 0)
    ocol = lax.broadcasted_iota(jnp.int32, (rows, rows), 1)
    s_own = _dot_nt(q, kn_ref[0, 0][:, 0:HEAD_DIM].astype(BF16))
    s_own = jnp.where((ocol <= orow) & (ocol < dec_seq), s_own, -jnp.inf)

    m = jnp.max(s_own, axis=1, keepdims=True)
    for s in ss:
        m = jnp.maximum(m, jnp.max(s, axis=1, keepdims=True))
    p_own = jnp.exp(s_own - m)
    den = jnp.sum(p_own, axis=1, keepdims=True)
    acc = _dot(p_own.astype(BF16), vn_ref[0, 0][:, 0:HEAD_DIM].astype(BF16))
    for n in range(n_sel_pages):
        p = jnp.exp(ss[n] - m)
        den = den + jnp.sum(p, axis=1, keepdims=True)
        acc = acc + _dot_nt(p.astype(BF16), vp_refs[n][0, 0, 0].astype(BF16))
    val = acc / den
    val2 = jnp.concatenate([val, val], axis=1)

    @pl.when(odd == 0)
    def _():
        o_ref[0] = val2

    @pl.when(odd == 1)
    def _():
        o_ref[0] = jnp.where(lane < HEAD_DIM, o_ref[0], val2)


def _attention_sample(pg_flat, q8, kn8, vn8, cache_kt, cache_vt, *, layer, nb, dec_seq):
    ppb = MOBA_BLOCK // PAGE_SIZE
    rows = q8.shape[2]
    n_sel_pages = dec_seq * MOBA_TOPK * ppb

    def page_spec(n):
        return pl.BlockSpec((1, 1, 1, HEAD_DIM, PAGE_SIZE),
                            lambda b, h, pg: (layer, pg[(b * ATT_HEADS + h) * n_sel_pages + n], h, 0, 0))

    head = pl.BlockSpec((1, 1, rows, LANES), lambda b, h, pg: (h, b, 0, 0))
    grid_spec = pltpu.PrefetchScalarGridSpec(
        num_scalar_prefetch=1,
        grid=(nb, ATT_HEADS),
        in_specs=[head, head, head] + [page_spec(n) for n in range(n_sel_pages)] * 2,
        out_specs=pl.BlockSpec((1, rows, LANES), lambda b, h, pg: (b, 0, h // 2)),
    )
    return pl.pallas_call(
        functools.partial(_attn_sample_kernel, dec_seq=dec_seq, n_sel_pages=n_sel_pages),
        grid_spec=grid_spec,
        out_shape=jax.ShapeDtypeStruct((nb, rows, ATT_WIDTH), F32),
        compiler_params=pltpu.CompilerParams(dimension_semantics=("arbitrary", "arbitrary")),
        name="attention_sample",
    )(pg_flat, q8, kn8, vn8, *([cache_kt] * n_sel_pages), *([cache_vt] * n_sel_pages))


def _rope_tables(pos):
    half = ROT_DIM // 2
    inv = jnp.exp(-math.log(ROPE_THETA) * jnp.arange(half, dtype=F32) * (2.0 / ROT_DIM))
    ang = pos.astype(F32)[:, None] * inv[None, :]
    cos, sin = jnp.cos(ang), jnp.sin(ang)
    n = pos.shape[0]
    pad = jnp.zeros((n, HEAD_DIM - ROT_DIM), F32)
    zero = jnp.zeros((n, half), F32)
    cos_h = jnp.concatenate([cos, cos, pad + 1.0], axis=1)
    s1_h = jnp.concatenate([-sin, zero, pad], axis=1)
    s2_h = jnp.concatenate([zero, sin, pad], axis=1)
    two = lambda t: jnp.concatenate([t, t], axis=1)
    return two(cos_h), two(s1_h), two(s2_h)


def _pad_seq_rows(a, nbs, dec_seq, rows):
    a = a.reshape(a.shape[0], nbs, dec_seq, a.shape[-1])
    return jnp.pad(a, ((0, 0), (0, 0), (0, rows - dec_seq), (0, 0)))


def _sample_attend(q_hm, k_hm, v_hm, kmean_l, pt_flat, cache_kt, cache_vt, *, layer, nbs, dec_seq, n_pages):
    q8 = _pad_seq_rows(q_hm, nbs, dec_seq, 8)
    k8 = _pad_seq_rows(k_hm, nbs, dec_seq, 8)
    v8 = _pad_seq_rows(v_hm, nbs, dec_seq, 8)
    idx = _topk_sample(q8, kmean_l)
    ppb = MOBA_BLOCK // PAGE_SIZE
    logical = idx[:, :, :dec_seq, :MOBA_TOPK, None] * ppb + jnp.arange(ppb, dtype=jnp.int32)
    pg_flat = jnp.take_along_axis(pt_flat.reshape(nbs, n_pages), logical.reshape(nbs, -1), axis=1).reshape(-1)
    o = _attention_sample(pg_flat, q8, k8, v8, cache_kt, cache_vt, layer=layer, nb=nbs, dec_seq=dec_seq)
    return o[:, :dec_seq].reshape(nbs * dec_seq, ATT_WIDTH)


def kernel(x_prompt, x_sample, cache_k, cache_v, state_conv, state_ffn, page_table, c_prompt, c_sample,
           norm1_g, norm2_g, w_ada, b_ada, w_in, q_norm_g, k_norm_g, conv_w, conv_b, conv_ln_g, conv_ln_b,
           attn_out_g, conv_out_g, w_out, w_fc1, ffn_conv_w, ffn_conv_b, w_fc2):
    nb, seq, _ = x_prompt.shape
    nbs, dec_seq, _ = x_sample.shape
    depth = w_in.shape[0]
    n_pages = page_table.shape[1]
    past = n_pages * PAGE_SIZE
    n_pool = cache_k.shape[1]
    n_rows_s = nbs * dec_seq
    assert past % MOBA_BLOCK == 0 and past // MOBA_BLOCK >= MOBA_TOPK
    assert seq % TM_IN == 0 and seq % TM_FFN == 0 and TM_IN % MOBA_BLOCK == 0 and n_pages % PAGES_PER_STEP == 0

    n_c = nb + nbs
    c_rows = -(-n_c // 8) * 8
    c_all = jnp.pad(jnp.concatenate([c_prompt, c_sample], axis=0), ((0, c_rows - n_c), (0, 0)))
    m_all = _modulation(c_all, w_ada, b_ada)

    w_in_bf, w_out_bf = w_in.astype(BF16), w_out.astype(BF16)
    w_fc1_bf, w_fc2_bf = w_fc1.astype(BF16), w_fc2.astype(BF16)
    head_id = jnp.arange(ATT_WIDTH) // HEAD_DIM
    bd = jnp.where(head_id[:, None] == head_id[None, :], 1.0 / HEAD_DIM, 0.0).astype(BF16)

    tabs_p = _rope_tables(jnp.arange(seq))
    tabs_s = _rope_tables(past + (jnp.arange(n_rows_s) % dec_seq))

    cache_kt = cache_k.transpose(0, 1, 3, 4, 2)
    cache_vt = cache_v.transpose(0, 1, 3, 4, 2)
    pt_flat = page_table.reshape(-1)
    kmean_s = _page_means(cache_kt, pt_flat, nb=nbs, n_pages=n_pages)

    xp = x_prompt.reshape(nb * seq, D_MODEL)
    xs = x_sample.reshape(n_rows_s, D_MODEL)
    outs = [[] for _ in range(8)]
    for l in range(depth):
        row = lambda a: a[l].reshape(1, -1)
        mp = m_all[l, :nb].reshape(nb, N_MOD, 1, D_MODEL)
        ms = jnp.repeat(m_all[l, nb:n_c].reshape(nbs, N_MOD, D_MODEL), dec_seq, axis=0)
        mp = [mp[:, i] for i in range(N_MOD)]
        ms = [ms[:, i] for i in range(N_MOD)]
        qg = jnp.tile(q_norm_g[l], ATT_HEADS).reshape(1, -1)
        kg = jnp.tile(k_norm_g[l], ATT_HEADS).reshape(1, -1)

        k, v, u, kaug, vaug, qaug = _mixer_in(
            xp, mp[0], mp[1], row(norm1_g), w_in_bf[l], bd, qg, kg, *tabs_p, prompt=True, tm=TM_IN, seq=seq)
        att = _attention_prompt(qaug, kaug, vaug, nb=nb, seq=seq)
        cvn = _conv_prompt(u, conv_w[l], row(conv_b), row(conv_ln_g), row(conv_ln_b), row(conv_out_g),
                           nb=nb, seq=seq)
        xp_new, ffn_state_p = _ffn_prompt(
            xp, att.reshape(nb * seq, ATT_WIDTH), cvn.reshape(nb * seq, CONV_WIDTH),
            mp[2], mp[3], mp[4], mp[5], row(attn_out_g), row(norm2_g),
            w_out_bf[l], w_fc1_bf[l], ffn_conv_w[l], row(ffn_conv_b), w_fc2_bf[l], nb=nb, seq=seq)
        outs[0].append(k.transpose(0, 3, 1, 2))
        outs[1].append(v.transpose(0, 3, 1, 2))
        outs[2].append(u.reshape(nb, seq, CONV_WIDTH)[:, seq - (CONV_K - 1):])
        outs[3].append(ffn_state_p)
        xp = xp_new

        ks, vs, us, q_hm, k_hm, v_hm = _mixer_in(
            xs, ms[0][None], ms[1][None], row(norm1_g), w_in_bf[l], bd, qg, kg, *tabs_s,
            prompt=False, tm=n_rows_s, seq=n_rows_s)
        att_s = _sample_attend(q_hm, k_hm, v_hm, kmean_s[l], pt_flat, cache_kt, cache_vt,
                               layer=l, nbs=nbs, dec_seq=dec_seq, n_pages=n_pages)
        u_ext = jnp.concatenate([state_conv[l], us.reshape(nbs, dec_seq, CONV_WIDTH)], axis=1)
        cvn_s = _conv_sample(u_ext, conv_w[l], row(conv_b), row(conv_ln_g), row(conv_ln_b), row(conv_out_g))
        st = state_ffn[l]
        pm1 = jnp.repeat(st[:, 1], dec_seq, axis=0)
        pm2 = jnp.stack([st[:, 0], st[:, 1]] + [st[:, 1]] * (dec_seq - 2), axis=1).reshape(n_rows_s, D_FF)
        xs_new, g_s = _ffn_sample(
            xs, att_s, cvn_s.reshape(n_rows_s, CONV_WIDTH), ms[2], ms[3], ms[4], ms[5],
            row(attn_out_g), row(norm2_g), w_out_bf[l], w_fc1_bf[l], ffn_conv_w[l], row(ffn_conv_b),
            w_fc2_bf[l], pm1, pm2, seq=dec_seq)
        outs[4].append(ks.reshape(nbs, dec_seq, ATT_HEADS, HEAD_DIM))
        outs[5].append(vs.reshape(nbs, dec_seq, ATT_HEADS, HEAD_DIM))
        outs[6].append(u_ext[:, dec_seq:])
        outs[7].append(g_s.reshape(nbs, dec_seq, D_FF)[:, dec_seq - (FFN_CONV_K - 1):])
        xs = xs_new

    stk = [jnp.stack(o) for o in outs]
    return (xp.reshape(nb, seq, D_MODEL), xs.reshape(nbs, dec_seq, D_MODEL),
            stk[0], stk[1], stk[2], stk[3], stk[4], stk[5], stk[6], stk[7])
```
